```python
import math
import jax, jax.numpy as jnp
from jax import lax
import numpy as np

D_MODEL = 2048
BATCH = 4
SEQ = 8192
DEPTH = 4

CHUNK = 64
Q_BLOCK = 128
N_MIXERS = 3
N_HEADS = 16
HEAD_DIM = D_MODEL // N_HEADS
ROPE_DIM = HEAD_DIM // 4
ROPE_THETA = 500000.0
B_KV_HEADS = 4
IDX_HEADS = 16
IDX_DIM = 64
IDX_ROPE_DIM = IDX_DIM // 4
TOPK_MAX = 256
D_FF = 4 * D_MODEL
PLE_DIM = 256
RMS_EPS = 1e-6
N_A = (DEPTH + 2) // 3
N_B = (DEPTH + 1) // 3
N_C = DEPTH // 3
B_SPLITS = (N_HEADS * HEAD_DIM, B_KV_HEADS * HEAD_DIM, B_KV_HEADS * HEAD_DIM,
            IDX_HEADS * IDX_DIM, IDX_DIM, IDX_HEADS)
B_IN_WIDTH = sum(B_SPLITS)

kernel_name = 'hybrid_stickbreak_dsa_fox_trunk'


def rmsnorm(x, g):
    xf = x.astype(jnp.float32)
    y = xf * lax.rsqrt(jnp.mean(xf * xf, axis=-1, keepdims=True) + RMS_EPS)
    return (y * g.astype(jnp.float32)).astype(x.dtype)


def rope_partial(x, pos, rot_dim):
    half = rot_dim // 2
    inv = ROPE_THETA ** (-jnp.arange(half, dtype=jnp.float32) / half)
    ang = pos.astype(jnp.float32)[:, None] * inv[None, :]
    cos = jnp.cos(ang)[None, :, None, :]
    sin = jnp.sin(ang)[None, :, None, :]
    xr = x[..., :rot_dim].astype(jnp.float32)
    x1, x2 = xr[..., :half], xr[..., half:]
    rot = jnp.concatenate([x1 * cos - x2 * sin, x1 * sin + x2 * cos], axis=-1)
    return jnp.concatenate([rot.astype(x.dtype), x[..., rot_dim:]], axis=-1)


def heads(t, n):
    return t.reshape(t.shape[0], t.shape[1], n, -1)


def stick_breaking_attention(h, w_in, q_gain, k_gain):
    bsz, seq, _ = h.shape
    q, k, v = jnp.split(h @ w_in, 3, axis=-1)
    q = rmsnorm(heads(q, N_HEADS), q_gain) * (HEAD_DIM ** -0.5)
    k = rmsnorm(heads(k, N_HEADS), k_gain)
    v = heads(v, N_HEADS)
    outs = []
    for qb in range(seq // Q_BLOCK):
        q0, end = qb * Q_BLOCK, (qb + 1) * Q_BLOCK
        z = jnp.einsum('bthd,bshd->bhts', q[:, q0:end], k[:, :end]).astype(jnp.float32)
        strict = jnp.arange(end)[None, :] < jnp.arange(q0, end)[:, None]
        log_1mb = jnp.where(strict, jax.nn.log_sigmoid(-z), 0.0)
        after = lax.cumsum(log_1mb, axis=3, reverse=True) - log_1mb
        wts = jnp.where(strict, jnp.exp(jax.nn.log_sigmoid(z) + after), 0.0)
        outs.append(jnp.einsum('bhts,bshd->bthd', wts.astype(v.dtype), v[:, :end]))
    return jnp.concatenate(outs, axis=1).reshape(bsz, seq, N_HEADS * HEAD_DIM)


def dsa_sparse_attention(h, w_in, q_gain, k_gain, idx_k_gain):
    bsz, seq, _ = h.shape
    offs = [int(o) for o in np.cumsum(B_SPLITS)[:-1]]
    q, k, v, qi, ki, wi = jnp.split(h @ w_in, offs, axis=-1)
    pos = jnp.arange(seq)
    q = rope_partial(rmsnorm(heads(q, N_HEADS), q_gain), pos, ROPE_DIM) * (HEAD_DIM ** -0.5)
    k = rope_partial(rmsnorm(heads(k, B_KV_HEADS), k_gain), pos, ROPE_DIM)
    v = heads(v, B_KV_HEADS)
    qi = rope_partial(heads(qi, IDX_HEADS), pos, IDX_ROPE_DIM)
    ki = rope_partial(rmsnorm(ki, idx_k_gain)[:, :, None, :], pos, IDX_ROPE_DIM)[:, :, 0]
    wi = wi.astype(jnp.float32) * (IDX_HEADS ** -0.5) * (IDX_DIM ** -0.5)
    k_top = min(TOPK_MAX, seq // 4)
    group = N_HEADS // B_KV_HEADS
    b_idx = jnp.arange(bsz)[:, None, None]
    outs = []
    for qb in range(seq // Q_BLOCK):
        q0, end = qb * Q_BLOCK, (qb + 1) * Q_BLOCK
        width = min(seq, max(end, k_top))
        chunk_end = (jnp.arange(q0, end) // CHUNK + 1) * CHUNK
        admiss = jnp.arange(width)[None, :] < chunk_end[:, None]
        idx_logits = jnp.einsum('bthd,bsd->bths', qi[:, q0:end], ki[:, :width]).astype(jnp.float32)
        score = jnp.einsum('bth,bths->bts', wi[:, q0:end], jax.nn.relu(idx_logits))
        score = jnp.where(admiss[None], score, -jnp.inf)
        _, sel = lax.top_k(score, k_top)
        valid = sel < chunk_end[None, :, None]
        k_sel = k[b_idx, sel]
        v_sel = v[b_idx, sel]
        qg = q[:, q0:end].reshape(bsz, Q_BLOCK, B_KV_HEADS, group, HEAD_DIM)
        logits = jnp.einsum('btgrd,btkgd->btgrk', qg, k_sel).astype(jnp.float32)
        logits = jnp.where(valid[:, :, None, None, :], logits, -jnp.inf)
        probs = jax.nn.softmax(logits, axis=-1)
        o = jnp.einsum('btgrk,btkgd->btgrd', probs.astype(v.dtype), v_sel)
        outs.append(o.reshape(bsz, Q_BLOCK, N_HEADS * HEAD_DIM))
    return jnp.concatenate(outs, axis=1)


def forgetting_attention(h, w_in, b_f, q_gain, k_gain):
    bsz, seq, _ = h.shape
    dm = N_HEADS * HEAD_DIM
    proj = h @ w_in
    q = rmsnorm(heads(proj[..., :dm], N_HEADS), q_gain) * (HEAD_DIM ** -0.5)
    k = rmsnorm(heads(proj[..., dm:2 * dm], N_HEADS), k_gain)
    v = heads(proj[..., 2 * dm:3 * dm], N_HEADS)
    log_f = jax.nn.log_sigmoid(proj[..., 3 * dm:].astype(jnp.float32) + b_f.astype(jnp.float32))
    cum = jnp.cumsum(log_f, axis=1).transpose(0, 2, 1)
    outs = []
    for qb in range(seq // Q_BLOCK):
        q0, end = qb * Q_BLOCK, (qb + 1) * Q_BLOCK
        logits = jnp.einsum('bthd,bshd->bhts', q[:, q0:end], k[:, :end]).astype(jnp.float32)
        logits = logits + cum[:, :, q0:end, None] - cum[:, :, None, :end]
        causal = jnp.arange(end)[None, :] <= jnp.arange(q0, end)[:, None]
        probs = jax.nn.softmax(jnp.where(causal, logits, -jnp.inf), axis=-1)
        outs.append(jnp.einsum('bhts,bshd->bthd', probs.astype(v.dtype), v[:, :end]))
    return jnp.concatenate(outs, axis=1).reshape(bsz, seq, dm)


def setup_inputs(seed: int = 0) -> dict:
    key = jax.random.key(seed)
    ks = jax.random.split(key, 24)
    f32 = jnp.float32
    D = D_MODEL

    def nrm(k, shape, scale):
        return jax.random.normal(k, shape, f32) * scale

    def gain(k, shape):
        return 1.0 + 0.02 * jax.random.normal(k, shape, f32)

    return {
        'x': nrm(ks[0], (BATCH, SEQ, D), 1.0),
        'p': nrm(ks[1], (DEPTH, BATCH, SEQ, PLE_DIM), 1.0),
        'attn_norm': gain(ks[2], (DEPTH, D)),
        'w_o': nrm(ks[3], (DEPTH, D, D), D ** -0.5),
        'mlp_norm': gain(ks[4], (DEPTH, D)),
        'w_up': nrm(ks[5], (DEPTH, D, D_FF), D ** -0.5),
        'w_down': nrm(ks[6], (DEPTH, D_FF, D), 0.5 * D_FF ** -0.5),
        'ple_norm': gain(ks[7], (DEPTH, D)),
        'w_ple_gate': nrm(ks[8], (DEPTH, D, D), D ** -0.5),
        'w_ple_proj': nrm(ks[9], (DEPTH, PLE_DIM, D), PLE_DIM ** -0.5),
        'a_w_in': nrm(ks[10], (N_A, D, 3 * D), D ** -0.5),
        'a_q_norm': gain(ks[11], (N_A, HEAD_DIM)),
        'a_k_norm': gain(ks[12], (N_A, HEAD_DIM)),
        'b_w_in': nrm(ks[13], (N_B, D, B_IN_WIDTH), D ** -0.5),
        'b_q_norm': gain(ks[14], (N_B, HEAD_DIM)),
        'b_k_norm': gain(ks[15], (N_B, HEAD_DIM)),
        'b_idx_k_norm': gain(ks[16], (N_B, IDX_DIM)),
        'c_w_in': nrm(ks[17], (N_C, D, 3 * D + N_HEADS), D ** -0.5),
        'c_b_f': jax.random.uniform(ks[18], (N_C, N_HEADS), f32, 1.0, 5.0),
        'c_q_norm': gain(ks[19], (N_C, HEAD_DIM)),
        'c_k_norm': gain(ks[20], (N_C, HEAD_DIM)),
    }


def reference(x, p, attn_norm, w_o, mlp_norm, w_up, w_down, ple_norm, w_ple_gate,
              w_ple_proj, a_w_in, a_q_norm, a_k_norm, b_w_in, b_q_norm, b_k_norm,
              b_idx_k_norm, c_w_in, c_b_f, c_q_norm, c_k_norm):
    h = x
    for i in range(DEPTH):
        kind, j = i % N_MIXERS, i // N_MIXERS
        hn = rmsnorm(h, attn_norm[i])
        if kind == 0:
            mix = stick_breaking_attention(hn, a_w_in[j], a_q_norm[j], a_k_norm[j])
        elif kind == 1:
            mix = dsa_sparse_attention(hn, b_w_in[j], b_q_norm[j], b_k_norm[j], b_idx_k_norm[j])
        else:
            mix = forgetting_attention(hn, c_w_in[j], c_b_f[j], c_q_norm[j], c_k_norm[j])
        h = h + mix @ w_o[i]
        hn = rmsnorm(h, mlp_norm[i])
        h = h + jnp.square(jax.nn.relu(hn @ w_up[i])) @ w_down[i]
        gate = jax.nn.sigmoid(rmsnorm(h, ple_norm[i]) @ w_ple_gate[i])
        h = h + gate * (p[i] @ w_ple_proj[i])
    return h
```

```python
import functools
import math

import jax
import jax.numpy as jnp
from jax import lax
from jax.experimental import pallas as pl
from jax.experimental.pallas import tpu as pltpu

F32 = jnp.float32
BF16 = jnp.bfloat16

N_HEADS = 16
HEAD_DIM = 128
ROPE_DIM = HEAD_DIM // 4
ROPE_THETA = 500000.0
B_KV_HEADS = 4
IDX_HEADS = 16
IDX_DIM = 64
IDX_ROPE_DIM = IDX_DIM // 4
TOPK_MAX = 256
CHUNK = 64
RMS_EPS = 1e-6
N_MIXERS = 3

LANES = 128
VMEM_LIMIT = 56 * 1024 * 1024
NEG_BIG = -1e30
EXP_UNDERFLOW = -104.0
INT_MIN = -(2 ** 31)
KEY_NEG_INF = INT_MIN + 0x7FFFFF


def _cparams(sem):
    return pltpu.CompilerParams(dimension_semantics=sem, vmem_limit_bytes=VMEM_LIMIT)


def _nt_dot(a, b):
    return lax.dot_general(a, b, (((1,), (1,)), ((), ())), preferred_element_type=F32)


def _dot(a, b):
    return jnp.dot(a, b, preferred_element_type=F32)


def _norm_rows(x_ref, g_ref, xn_ref, rows=256):
    tm = x_ref.shape[0]
    rows = min(rows, tm)
    for r in range(0, tm, rows):
        x = x_ref[r:r + rows, :]
        ms = jnp.mean(x * x, axis=-1, keepdims=True)
        xn_ref[r:r + rows, :] = (x * lax.rsqrt(ms + RMS_EPS) * g_ref[...]).astype(BF16)


def _head_norm(blk, gain_row):
    ms = jnp.mean(blk * blk, axis=-1, keepdims=True)
    return blk * lax.rsqrt(ms + RMS_EPS) * gain_row


def _rope(y, c, s_up, s_dn, half):
    return y * c + pltpu.roll(y, LANES - half, 1) * s_up + pltpu.roll(y, half, 1) * s_dn


def _proj_kernel(*refs, sections, rope_half):
    if rope_half:
        x_ref, g_ref, w_ref, hg_ref, c_ref, su_ref, sd_ref, o_ref, xn_ref = refs
    else:
        x_ref, g_ref, w_ref, hg_ref, o_ref, xn_ref = refs
    j = pl.program_id(1)

    @pl.when(j == 0)
    def _():
        _norm_rows(x_ref, g_ref, xn_ref)

    acc = _dot(xn_ref[...], w_ref[...])
    tn = acc.shape[1]
    for lo, hi, gain_row, use_rope in sections:
        @pl.when((j >= lo) & (j < hi))
        def _(gain_row=gain_row, use_rope=use_rope):
            if gain_row is None:
                o_ref[...] = acc.astype(o_ref.dtype)
                return
            for c in range(tn // LANES):
                y = _head_norm(acc[:, c * LANES:(c + 1) * LANES], hg_ref[gain_row:gain_row + 1, :])
                if use_rope:
                    y = _rope(y, c_ref[...], su_ref[...], sd_ref[...], rope_half)
                o_ref[:, c * LANES:(c + 1) * LANES] = y.astype(o_ref.dtype)


def _proj(h2, gain, w, head_gains, sections, tn, rope_tabs=None, rope_half=0, seq=None, tm=512):
    m, d = h2.shape
    n = w.shape[1]
    tm = min(tm, m)
    in_specs = [
        pl.BlockSpec((tm, d), lambda i, j: (i, 0)),
        pl.BlockSpec((1, d), lambda i, j: (0, 0)),
        pl.BlockSpec((d, tn), lambda i, j: (0, j)),
        pl.BlockSpec(head_gains.shape, lambda i, j: (0, 0)),
    ]
    args = [h2, gain.reshape(1, d), w, head_gains]
    if rope_half:
        tiles_per_seq = seq // tm
        for t in rope_tabs:
            in_specs.append(pl.BlockSpec((tm, LANES), lambda i, j: (i % tiles_per_seq, 0)))
            args.append(t)
    return pl.pallas_call(
        functools.partial(_proj_kernel, sections=sections, rope_half=rope_half),
        grid=(m // tm, n // tn),
        in_specs=in_specs,
        out_specs=pl.BlockSpec((tm, tn), lambda i, j: (i, j)),
        out_shape=jax.ShapeDtypeStruct((m, n), BF16),
        scratch_shapes=[pltpu.VMEM((tm, d), BF16)],
        compiler_params=_cparams(("parallel", "arbitrary")),
        name="proj",
    )(*args)


def _proj_idx_kernel(x_ref, g_ref, w_ref, kg_ref, ci_ref, sui_ref, sdi_ref, ct_ref, sut_ref, sdt_ref,
                     qi_ref, tail_ref, xn_ref, *, wi_scale):
    _norm_rows(x_ref, g_ref, xn_ref)
    acc = _dot(xn_ref[...], w_ref[...])
    half = IDX_ROPE_DIM // 2
    n_qi = qi_ref.shape[1]
    for c in range(n_qi // LANES):
        y = _rope(acc[:, c * LANES:(c + 1) * LANES], ci_ref[...], sui_ref[...], sdi_ref[...], half)
        qi_ref[:, c * LANES:(c + 1) * LANES] = y.astype(qi_ref.dtype)
    t = acc[:, n_qi:n_qi + LANES]
    is_k = lax.broadcasted_iota(jnp.int32, t.shape, 1) < IDX_DIM
    ms = jnp.sum(jnp.where(is_k, t * t, 0.0), axis=-1, keepdims=True) * (1.0 / IDX_DIM)
    y = jnp.where(is_k, t * lax.rsqrt(ms + RMS_EPS) * kg_ref[...], t * wi_scale)
    tail_ref[...] = _rope(y, ct_ref[...], sut_ref[...], sdt_ref[...], half)


def _proj_idx(h2, gain, w, kgain_pad, tabs, seq, tm=512):
    m, d = h2.shape
    n = w.shape[1]
    n_qi = IDX_HEADS * IDX_DIM
    tm = min(tm, m)
    tiles_per_seq = seq // tm
    const = lambda i: (0, 0)
    tab_spec = pl.BlockSpec((tm, LANES), lambda i: (i % tiles_per_seq, 0))
    return pl.pallas_call(
        functools.partial(_proj_idx_kernel, wi_scale=(IDX_HEADS ** -0.5) * (IDX_DIM ** -0.5)),
        grid=(m // tm,),
        in_specs=[pl.BlockSpec((tm, d), lambda i: (i, 0)), pl.BlockSpec((1, d), const),
                  pl.BlockSpec((d, n), const), pl.BlockSpec((1, LANES), const)] + [tab_spec] * 6,
        out_specs=[pl.BlockSpec((tm, n_qi), lambda i: (i, 0)), pl.BlockSpec((tm, LANES), lambda i: (i, 0))],
        out_shape=[jax.ShapeDtypeStruct((m, n_qi), BF16), jax.ShapeDtypeStruct((m, LANES), F32)],
        scratch_shapes=[pltpu.VMEM((tm, d), BF16)],
        compiler_params=_cparams(("parallel",)),
        name="proj_idx",
    )(h2, gain.reshape(1, d), w, kgain_pad, *tabs)


def _fgate_kernel(x_ref, g_ref, w_ref, b_ref, o_ref, xn_ref, carry_ref, *, tiles_per_seq):
    i = pl.program_id(0)
    _norm_rows(x_ref, g_ref, xn_ref)
    y = _dot(xn_ref[...], w_ref[...]) + b_ref[...]
    lf = jnp.minimum(y, 0.0) - jnp.log1p(jnp.exp(-jnp.abs(y)))
    tm = lf.shape[0]
    tri = (lax.broadcasted_iota(jnp.int32, (tm, tm), 0) >= lax.broadcasted_iota(jnp.int32, (tm, tm), 1))
    tri = jnp.where(tri, 1.0, 0.0).astype(BF16)
    a = lf.astype(BF16)
    r1 = lf - a.astype(F32)
    b = r1.astype(BF16)
    c = (r1 - b.astype(F32)).astype(BF16)
    cs = _dot(tri, a) + _dot(tri, b) + _dot(tri, c)

    @pl.when(i % tiles_per_seq == 0)
    def _():
        carry_ref[...] = jnp.zeros_like(carry_ref)

    out = cs + carry_ref[...]
    o_ref[...] = out
    carry_ref[...] = out[tm - 1:tm, :]


def _fgate_cumsum(h2, gain, w_pad, b_pad, seq, tm=512):
    m, d = h2.shape
    tm = min(tm, seq)
    const = lambda i: (0, 0)
    return pl.pallas_call(
        functools.partial(_fgate_kernel, tiles_per_seq=seq // tm),
        grid=(m // tm,),
        in_specs=[pl.BlockSpec((tm, d), lambda i: (i, 0)), pl.BlockSpec((1, d), const),
                  pl.BlockSpec((d, LANES), const), pl.BlockSpec((1, LANES), const)],
        out_specs=pl.BlockSpec((tm, LANES), lambda i: (i, 0)),
        out_shape=jax.ShapeDtypeStruct((m, LANES), F32),
        scratch_shapes=[pltpu.VMEM((tm, d), BF16), pltpu.VMEM((1, LANES), F32)],
        compiler_params=_cparams(("arbitrary",)),
        name="fgate_cumsum",
    )(h2, gain.reshape(1, d), w_pad, b_pad)


def _oproj_kernel(h_ref, mix_ref, w_ref, o_ref):
    o_ref[...] = h_ref[...] + _dot(mix_ref[...], w_ref[...])


def _oproj(h2, mix2, w, tm=512):
    m, d = h2.shape
    tm = min(tm, m)
    return pl.pallas_call(
        _oproj_kernel,
        grid=(m // tm,),
        in_specs=[pl.BlockSpec((tm, d), lambda i: (i, 0)), pl.BlockSpec((tm, d), lambda i: (i, 0)),
                  pl.BlockSpec((d, d), lambda i: (0, 0))],
        out_specs=pl.BlockSpec((tm, d), lambda i: (i, 0)),
        out_shape=jax.ShapeDtypeStruct((m, d), F32),
        compiler_params=_cparams(("parallel",)),
        name="oproj",
    )(h2, mix2, w)


def _mlp_kernel(x_ref, g_ref, wu_ref, wd_ref, o_ref, xn_ref):
    f = pl.program_id(1)

    @pl.when(f == 0)
    def _():
        _norm_rows(x_ref, g_ref, xn_ref)
        o_ref[...] = x_ref[...]

    a = jnp.maximum(_dot(xn_ref[...], wu_ref[...]), 0.0)
    o_ref[...] += _dot((a * a).astype(BF16), wd_ref[...])


def _mlp(h2, gain, w_up, w_down, tm=512, tf=512):
    m, d = h2.shape
    dff = w_up.shape[1]
    tm = min(tm, m)
    return pl.pallas_call(
        _mlp_kernel,
        grid=(m // tm, dff // tf),
        in_specs=[pl.BlockSpec((tm, d), lambda i, f: (i, 0)), pl.BlockSpec((1, d), lambda i, f: (0, 0)),
                  pl.BlockSpec((d, tf), lambda i, f: (0, f)), pl.BlockSpec((tf, d), lambda i, f: (f, 0))],
        out_specs=pl.BlockSpec((tm, d), lambda i, f: (i, 0)),
        out_shape=jax.ShapeDtypeStruct((m, d), F32),
        scratch_shapes=[pltpu.VMEM((tm, d), BF16)],
        compiler_params=_cparams(("parallel", "arbitrary")),
        name="mlp",
    )(h2, gain.reshape(1, d), w_up, w_down)


def _ple_kernel(x_ref, g_ref, wg_ref, p_ref, wp_ref, o_ref, xn_ref):
    _norm_rows(x_ref, g_ref, xn_ref)
    gate = jax.nn.sigmoid(_dot(xn_ref[...], wg_ref[...]))
    emb = _dot(p_ref[...].astype(BF16), wp_ref[...])
    o_ref[...] = x_ref[...] + gate * emb


def _ple(h2, gain, w_gate, p2, w_proj, tm=512):
    m, d = h2.shape
    pd = p2.shape[1]
    tm = min(tm, m)
    const = lambda i: (0, 0)
    return pl.pallas_call(
        _ple_kernel,
        grid=(m // tm,),
        in_specs=[pl.BlockSpec((tm, d), lambda i: (i, 0)), pl.BlockSpec((1, d), const),
                  pl.BlockSpec((d, d), const), pl.BlockSpec((tm, pd), lambda i: (i, 0)),
                  pl.BlockSpec((pd, d), const)],
        out_specs=pl.BlockSpec((tm, d), lambda i: (i, 0)),
        out_shape=jax.ShapeDtypeStruct((m, d), F32),
        scratch_shapes=[pltpu.VMEM((tm, d), BF16)],
        compiler_params=_cparams(("parallel",)),
        name="ple",
    )(h2, gain.reshape(1, d), w_gate, p2, w_proj)


def _sb_kernel(q_ref, k_ref, v_ref, o_ref, acc_ref, aft_ref, *, tq):
    i = pl.program_id(2)
    q = q_ref[0]
    row = lax.broadcasted_iota(jnp.int32, (tq, tq), 0)
    col = lax.broadcasted_iota(jnp.int32, (tq, tq), 1)
    later = jnp.where(row > col, 1.0, 0.0).astype(BF16)
    strict = col < row

    def block(j, diagonal):
        start = pl.multiple_of(j * tq, tq)
        kb = k_ref[0, pl.ds(start, tq), :]
        vb = v_ref[0, pl.ds(start, tq), :]
        z = _nt_dot(q, kb)
        sp = jnp.maximum(z, 0.0) + jnp.log1p(jnp.exp(-jnp.abs(z)))
        log_1mb = -sp
        if diagonal:
            log_1mb = jnp.where(strict, log_1mb, 0.0)
        hi = log_1mb.astype(BF16)
        lo = (log_1mb - hi.astype(F32)).astype(BF16)
        after = _dot(hi, later) + _dot(lo, later) + aft_ref[...]
        wts = jnp.exp((z - sp) + after)
        if diagonal:
            wts = jnp.where(strict, wts, 0.0)
        acc_ref[...] += _dot(wts.astype(BF16), vb)
        aft_ref[...] += jnp.sum(log_1mb, axis=-1, keepdims=True)

    acc_ref[...] = jnp.zeros_like(acc_ref)
    aft_ref[...] = jnp.zeros_like(aft_ref)
    block(i, True)

    def cond(c):
        j, worst = c
        return jnp.logical_and(j >= 0, worst > EXP_UNDERFLOW)

    def body(c):
        j, _ = c
        block(j, False)
        return j - 1, jnp.max(aft_ref[...])

    lax.while_loop(cond, body, (i - 1, jnp.max(aft_ref[...])))
    o_ref[0] = acc_ref[...].astype(o_ref.dtype)


def _sb_attention(qkv, tq=256):
    bsz, seq, _ = qkv.shape
    tq = min(tq, seq)
    return pl.pallas_call(
        functools.partial(_sb_kernel, tq=tq),
        grid=(bsz, N_HEADS, seq // tq),
        in_specs=[pl.BlockSpec((1, tq, HEAD_DIM), lambda b, h, i: (b, i, h)),
                  pl.BlockSpec((1, seq, HEAD_DIM), lambda b, h, i: (b, 0, N_HEADS + h)),
                  pl.BlockSpec((1, seq, HEAD_DIM), lambda b, h, i: (b, 0, 2 * N_HEADS + h))],
        out_specs=pl.BlockSpec((1, tq, HEAD_DIM), lambda b, h, i: (b, i, h)),
        out_shape=jax.ShapeDtypeStruct((bsz, seq, N_HEADS * HEAD_DIM), BF16),
        scratch_shapes=[pltpu.VMEM((tq, HEAD_DIM), F32), pltpu.VMEM((tq, 1), F32)],
        compiler_params=_cparams(("parallel", "parallel", "arbitrary")),
        name="sb_attn",
    )(qkv, qkv, qkv)


def _fox_kernel(q_ref, k_ref, v_ref, ct_ref, cs_ref, o_ref, acc_ref, m_ref, l_ref, *, tq):
    h = pl.program_id(1)
    i = pl.program_id(2)
    q = q_ref[0]
    lane = lax.broadcasted_iota(jnp.int32, (tq, LANES), 1)
    ct = jnp.sum(jnp.where(lane == h, ct_ref[0], 0.0), axis=-1, keepdims=True)
    causal = (lax.broadcasted_iota(jnp.int32, (tq, tq), 1) <= lax.broadcasted_iota(jnp.int32, (tq, tq), 0))

    def block(j, diagonal):
        start = pl.multiple_of(j * tq, tq)
        kb = k_ref[0, pl.ds(start, tq), :]
        vb = v_ref[0, pl.ds(start, tq), :]
        cs = cs_ref[0, 0, :, pl.ds(start, tq)]
        z = _nt_dot(q, kb) + ct - cs
        if diagonal:
            z = jnp.where(causal, z, NEG_BIG)
        m_old = m_ref[...]
        m_new = jnp.maximum(m_old, jnp.max(z, axis=-1, keepdims=True))
        alpha = jnp.exp(m_old - m_new)
        p = jnp.exp(z - m_new)
        l_ref[...] = alpha * l_ref[...] + jnp.sum(p, axis=-1, keepdims=True)
        acc_ref[...] = alpha * acc_ref[...] + _dot(p.astype(BF16), vb)
        m_ref[...] = m_new

    m_ref[...] = jnp.full_like(m_ref, NEG_BIG)
    l_ref[...] = jnp.zeros_like(l_ref)
    acc_ref[...] = jnp.zeros_like(acc_ref)

    def body(j, carry):
        block(j, False)
        return carry

    lax.fori_loop(0, i, body, 0)
    block(i, True)
    o_ref[0] = (acc_ref[...] / l_ref[...]).astype(o_ref.dtype)


def _fox_attention(qkv, cum, cum_t, tq=256):
    bsz, seq, _ = qkv.shape
    tq = min(tq, seq)
    return pl.pallas_call(
        functools.partial(_fox_kernel, tq=tq),
        grid=(bsz, N_HEADS, seq // tq),
        in_specs=[pl.BlockSpec((1, tq, HEAD_DIM), lambda b, h, i: (b, i, h)),
                  pl.BlockSpec((1, seq, HEAD_DIM), lambda b, h, i: (b, 0, N_HEADS + h)),
                  pl.BlockSpec((1, seq, HEAD_DIM), lambda b, h, i: (b, 0, 2 * N_HEADS + h)),
                  pl.BlockSpec((1, tq, LANES), lambda b, h, i: (b, i, 0)),
                  pl.BlockSpec((1, 1, 1, seq), lambda b, h, i: (b, h, 0, 0))],
        out_specs=pl.BlockSpec((1, tq, HEAD_DIM), lambda b, h, i: (b, i, h)),
        out_shape=jax.ShapeDtypeStruct((bsz, seq, N_HEADS * HEAD_DIM), BF16),
        scratch_shapes=[pltpu.VMEM((tq, HEAD_DIM), F32), pltpu.VMEM((tq, 1), F32), pltpu.VMEM((tq, 1), F32)],
        compiler_params=_cparams(("parallel", "parallel", "arbitrary")),
        name="fox_attn",
    )(qkv, qkv, qkv, cum, cum_t)


def _sortable(x):
    bits = pltpu.bitcast(x, jnp.int32)
    return jnp.where(bits < 0, bits ^ jnp.int32(0x7FFFFFFF), bits)


def _dsa_kernel(q_ref, k_ref, v_ref, qi_ref, ki_ref, wi_ref, o_ref,
                key_ref, wb_ref, thr_ref, acc_ref, m_ref, l_ref, *, tq, k_top, row_group):
    i = pl.program_id(1)
    n_blocks = i + 1
    n_sub = tq // LANES
    group = N_HEADS // B_KV_HEADS

    w = wi_ref[0]
    for h in range(IDX_HEADS):
        wb_ref[h] = jnp.broadcast_to(w[:, h:h + 1], (tq, LANES))

    row_t = i * tq + lax.broadcasted_iota(jnp.int32, (tq, LANES), 0)
    chunk_end = (row_t // CHUNK + 1) * CHUNK
    col = lax.broadcasted_iota(jnp.int32, (tq, LANES), 1)

    def score_block(j, carry):
        start = pl.multiple_of(j * tq, tq)
        ka = ki_ref[0, pl.ds(start, tq), 0:LANES]
        kb = ki_ref[0, pl.ds(start, tq), LANES:2 * LANES]
        score = [jnp.zeros((tq, LANES), F32) for _ in range(n_sub)]
        for mblk in range(IDX_HEADS // 2):
            qm = qi_ref[0, :, mblk * LANES:(mblk + 1) * LANES]
            for half, kk in enumerate((ka, kb)):
                logit = _nt_dot(qm, kk)
                wrow = wb_ref[2 * mblk + half]
                for c in range(n_sub):
                    score[c] = score[c] + jnp.maximum(logit[:, c * LANES:(c + 1) * LANES], 0.0) * wrow
        for c in range(n_sub):
            admissible = (start + c * LANES + col) < chunk_end
            key_ref[:, pl.ds(start + c * LANES, LANES)] = jnp.where(admissible, _sortable(score[c]), KEY_NEG_INF)
        return carry

    lax.fori_loop(0, n_blocks, score_block, 0)

    n_chunks = n_blocks * n_sub
    for rg in range(tq // row_group):
        rows = slice(rg * row_group, (rg + 1) * row_group)

        def bit_body(it, lo, rows=rows):
            cand = lo + jnp.left_shift(jnp.int32(1), 31 - it)

            def count_body(c, cnt):
                off = pl.multiple_of(c * LANES, LANES)
                return cnt + jnp.where(key_ref[rows, pl.ds(off, LANES)] >= cand, 1.0, 0.0)

            cnt = lax.fori_loop(0, n_chunks, count_body, jnp.zeros((row_group, LANES), F32))
            total = jnp.sum(cnt, axis=-1, keepdims=True)
            return jnp.where(total >= k_top, cand, lo)

        lo = lax.fori_loop(0, 32, bit_body, jnp.full((row_group, LANES), INT_MIN, jnp.int32))
        thr_ref[rows, :] = jnp.maximum(lo, KEY_NEG_INF + 1)

    m_ref[...] = jnp.full_like(m_ref, NEG_BIG)
    l_ref[...] = jnp.zeros_like(l_ref)
    acc_ref[...] = jnp.zeros_like(acc_ref)

    def attn_block(j, carry):
        start = pl.multiple_of(j * tq, tq)
        thr = thr_ref[...]
        bias = jnp.concatenate(
            [jnp.where(key_ref[:, pl.ds(start + c * LANES, LANES)] >= thr, 0.0, NEG_BIG) for c in range(n_sub)],
            axis=1)
        for g in range(B_KV_HEADS):
            kb = k_ref[0, pl.ds(start, tq), g * HEAD_DIM:(g + 1) * HEAD_DIM]
            vb = v_ref[0, pl.ds(start, tq), g * HEAD_DIM:(g + 1) * HEAD_DIM]
            for r in range(group):
                h = g * group + r
                z = _nt_dot(q_ref[0, :, h * HEAD_DIM:(h + 1) * HEAD_DIM], kb) + bias
                m_old = m_ref[h]
                m_new = jnp.maximum(m_old, jnp.max(z, axis=-1, keepdims=True))
                alpha = jnp.exp(m_old - m_new)
                p = jnp.exp(z - m_new)
                l_ref[h] = alpha * l_ref[h] + jnp.sum(p, axis=-1, keepdims=True)
                acc_ref[h] = alpha * acc_ref[h] + _dot(p.astype(BF16), vb)
                m_ref[h] = m_new
        return carry

    lax.fori_loop(0, n_blocks, attn_block, 0)
    for h in range(N_HEADS):
        o_ref[0, :, h * HEAD_DIM:(h + 1) * HEAD_DIM] = (acc_ref[h] / l_ref[h]).astype(o_ref.dtype)


def _dsa_attention(qkv, qi, ki2, wi, k_top, tq=256, row_group=64):
    bsz, seq, _ = qkv.shape
    tq = min(tq, seq)
    dq = N_HEADS * HEAD_DIM
    dkv = B_KV_HEADS * HEAD_DIM
    resident = dict(pipeline_mode=pl.Buffered(1))
    return pl.pallas_call(
        functools.partial(_dsa_kernel, tq=tq, k_top=k_top, row_group=min(row_group, tq)),
        grid=(bsz, seq // tq),
        in_specs=[pl.BlockSpec((1, tq, dq), lambda b, i: (b, i, 0)),
                  pl.BlockSpec((1, seq, dkv), lambda b, i: (b, 0, dq // dkv), **resident),
                  pl.BlockSpec((1, seq, dkv), lambda b, i: (b, 0, dq // dkv + 1), **resident),
                  pl.BlockSpec((1, tq, IDX_HEADS * IDX_DIM), lambda b, i: (b, i, 0)),
                  pl.BlockSpec((1, seq, 2 * LANES), lambda b, i: (b, 0, 0), **resident),
                  pl.BlockSpec((1, tq, IDX_HEADS), lambda b, i: (b, i, 0))],
        out_specs=pl.BlockSpec((1, tq, dq), lambda b, i: (b, i, 0)),
        out_shape=jax.ShapeDtypeStruct((bsz, seq, dq), BF16),
        scratch_shapes=[pltpu.VMEM((tq, seq), jnp.int32),
                        pltpu.VMEM((IDX_HEADS, tq, LANES), F32),
                        pltpu.VMEM((tq, LANES), jnp.int32),
                        pltpu.VMEM((N_HEADS, tq, HEAD_DIM), F32),
                        pltpu.VMEM((N_HEADS, tq, 1), F32),
                        pltpu.VMEM((N_HEADS, tq, 1), F32)],
        compiler_params=_cparams(("parallel", "arbitrary")),
        name="dsa_attn",
    )(qkv, qkv, qkv, qi, ki2, wi)


def _rope_tables(seq, rot_dim, period, active_lanes):
    half = rot_dim // 2
    inv = ROPE_THETA ** (-jnp.arange(half, dtype=F32) / half)
    ang = jnp.arange(seq, dtype=F32)[:, None] * inv[None, :]
    cos, sin = jnp.cos(ang), jnp.sin(ang)
    lane = jnp.arange(LANES)
    local = lane % period
    idx = local % half
    first = (local < half) & (lane < active_lanes)
    second = (local >= half) & (local < rot_dim) & (lane < active_lanes)
    c = jnp.where((first | second)[None, :], cos[:, idx], 1.0)
    s_up = jnp.where(first[None, :], -sin[:, idx], 0.0)
    s_dn = jnp.where(second[None, :], sin[:, idx], 0.0)
    return c, s_up, s_dn


def _pad_lanes(v, fill=0.0):
    return jnp.pad(v.astype(F32), (0, LANES - v.shape[0]), constant_values=fill).reshape(1, LANES)


def kernel(x, p, attn_norm, w_o, mlp_norm, w_up, w_down, ple_norm, w_ple_gate, w_ple_proj,
           a_w_in, a_q_norm, a_k_norm, b_w_in, b_q_norm, b_k_norm, b_idx_k_norm,
           c_w_in, c_b_f, c_q_norm, c_k_norm):
    bsz, seq, d = x.shape
    depth = attn_norm.shape[0]
    m = bsz * seq
    dm = N_HEADS * HEAD_DIM
    dkv = B_KV_HEADS * HEAD_DIM
    n_qi = IDX_HEADS * IDX_DIM
    scale = HEAD_DIM ** -0.5
    k_top = min(TOPK_MAX, seq // 4)

    h = x.reshape(m, d)
    for layer in range(depth):
        kind, j = layer % N_MIXERS, layer // N_MIXERS
        if kind == 0:
            gains = jnp.stack([a_q_norm[j] * scale, a_k_norm[j]])
            nt = dm // 1024
            qkv = _proj(h, attn_norm[layer], a_w_in[j].astype(BF16), gains,
                        ((0, nt, 0, False), (nt, 2 * nt, 1, False), (2 * nt, 3 * nt, None, False)), 1024)
            mix = _sb_attention(qkv.reshape(bsz, seq, 3 * dm))
        elif kind == 1:
            w = b_w_in[j]
            gains = jnp.stack([b_q_norm[j] * scale, b_k_norm[j]])
            main_tabs = _rope_tables(seq, ROPE_DIM, HEAD_DIM, LANES)
            tn = dkv
            qkv = _proj(h, attn_norm[layer], w[:, :dm + 2 * dkv].astype(BF16), gains,
                        ((0, dm // tn, 0, True), (dm // tn, dm // tn + 1, 1, True),
                         (dm // tn + 1, dm // tn + 2, None, False)),
                        tn, rope_tabs=main_tabs, rope_half=ROPE_DIM // 2, seq=seq)
            tail_w = IDX_DIM + IDX_HEADS
            w_idx = jnp.pad(w[:, dm + 2 * dkv:], ((0, 0), (0, LANES - tail_w))).astype(BF16)
            idx_tabs = _rope_tables(seq, IDX_ROPE_DIM, IDX_DIM, LANES)
            tail_tabs = _rope_tables(seq, IDX_ROPE_DIM, IDX_DIM, IDX_DIM)
            qi, tail = _proj_idx(h, attn_norm[layer], w_idx, _pad_lanes(b_idx_k_norm[j]),
                                 idx_tabs + tail_tabs, seq)
            ki = tail[:, :IDX_DIM].astype(BF16)
            zeros = jnp.zeros((m, LANES - IDX_DIM), BF16)
            ki2 = jnp.concatenate([ki, zeros, zeros, ki], axis=1).reshape(bsz, seq, 2 * LANES)
            wi = tail[:, IDX_DIM:tail_w].reshape(bsz, seq, IDX_HEADS)
            mix = _dsa_attention(qkv.reshape(bsz, seq, dm + 2 * dkv), qi.reshape(bsz, seq, n_qi), ki2, wi, k_top)
        else:
            w = c_w_in[j]
            gains = jnp.stack([c_q_norm[j] * scale, c_k_norm[j]])
            nt = dm // 1024
            qkv = _proj(h, attn_norm[layer], w[:, :3 * dm].astype(BF16), gains,
                        ((0, nt, 0, False), (nt, 2 * nt, 1, False), (2 * nt, 3 * nt, None, False)), 1024)
            w_f = jnp.pad(w[:, 3 * dm:], ((0, 0), (0, LANES - N_HEADS))).astype(BF16)
            cum = _fgate_cumsum(h, attn_norm[layer], w_f, _pad_lanes(c_b_f[j]), seq)
            cum3 = cum.reshape(bsz, seq, LANES)
            cum_t = jnp.transpose(cum3[:, :, :N_HEADS], (0, 2, 1)).reshape(bsz, N_HEADS, 1, seq)
            mix = _fox_attention(qkv.reshape(bsz, seq, 3 * dm), cum3, cum_t)
        h = _oproj(h, mix.reshape(m, dm), w_o[layer].astype(BF16))
        h = _mlp(h, mlp_norm[layer], w_up[layer].astype(BF16), w_down[layer].astype(BF16))
        h = _ple(h, ple_norm[layer], w_ple_gate[layer].astype(BF16), p[layer].reshape(m, -1),
                 w_ple_proj[layer].astype(BF16))
    return h.reshape(bsz, seq, d)
```

```python
import functools
import math

import jax
import jax.numpy as jnp
from jax import lax
from jax.experimental import pallas as pl
from jax.experimental.pallas import tpu as pltpu

F32 = jnp.float32
BF16 = jnp.bfloat16

N_HEADS = 16
HEAD_DIM = 128
ROPE_DIM = HEAD_DIM // 4
ROPE_THETA = 500000.0
B_KV_HEADS = 4
IDX_HEADS = 16
IDX_DIM = 64
IDX_ROPE_DIM = IDX_DIM // 4
TOPK_MAX = 256
CHUNK = 64
RMS_EPS = 1e-6
N_MIXERS = 3

LANES = 128
VMEM_LIMIT = 56 * 1024 * 1024
NEG_BIG = -1e30
EXP_UNDERFLOW = -104.0
LOG2E = math.log2(math.e)
ONES_ROWS = 16
INT_MIN = -(2 ** 31)
KEY_NEG_INF = INT_MIN + 0x7FFFFF


def _cparams(sem):
    return pltpu.CompilerParams(dimension_semantics=sem, vmem_limit_bytes=VMEM_LIMIT)


def _nt_dot(a, b):
    return lax.dot_general(a, b, (((1,), (1,)), ((), ())), preferred_element_type=F32)


def _dot(a, b):
    return jnp.dot(a, b, preferred_element_type=F32)


def _norm_rows(x_ref, g_ref, xn_ref, rows=256):
    tm = x_ref.shape[0]
    rows = min(rows, tm)
    for r in range(0, tm, rows):
        x = x_ref[r:r + rows, :]
        ms = jnp.mean(x * x, axis=-1, keepdims=True)
        xn_ref[r:r + rows, :] = (x * lax.rsqrt(ms + RMS_EPS) * g_ref[...]).astype(BF16)


def _head_norm(blk, gain_row):
    ms = jnp.mean(blk * blk, axis=-1, keepdims=True)
    return blk * lax.rsqrt(ms + RMS_EPS) * gain_row


def _rope(y, c, s_up, s_dn, half):
    return y * c + pltpu.roll(y, LANES - half, 1) * s_up + pltpu.roll(y, half, 1) * s_dn


def _proj_kernel(*refs, sections, rope_half):
    if rope_half:
        x_ref, g_ref, w_ref, hg_ref, c_ref, su_ref, sd_ref, o_ref, xn_ref = refs
    else:
        x_ref, g_ref, w_ref, hg_ref, o_ref, xn_ref = refs
    j = pl.program_id(1)

    @pl.when(j == 0)
    def _():
        _norm_rows(x_ref, g_ref, xn_ref)

    acc = _dot(xn_ref[...], w_ref[...])
    tn = acc.shape[1]
    for lo, hi, gain_row, use_rope in sections:
        @pl.when((j >= lo) & (j < hi))
        def _(gain_row=gain_row, use_rope=use_rope):
            if gain_row is None:
                o_ref[...] = acc.astype(o_ref.dtype)
                return
            for c in range(tn // LANES):
                y = _head_norm(acc[:, c * LANES:(c + 1) * LANES], hg_ref[gain_row:gain_row + 1, :])
                if use_rope:
                    y = _rope(y, c_ref[...], su_ref[...], sd_ref[...], rope_half)
                o_ref[:, c * LANES:(c + 1) * LANES] = y.astype(o_ref.dtype)


def _proj(h2, gain, w, head_gains, sections, tn, rope_tabs=None, rope_half=0, seq=None, tm=512):
    m, d = h2.shape
    n = w.shape[1]
    tm = min(tm, m)
    in_specs = [
        pl.BlockSpec((tm, d), lambda i, j: (i, 0)),
        pl.BlockSpec((1, d), lambda i, j: (0, 0)),
        pl.BlockSpec((d, tn), lambda i, j: (0, j)),
        pl.BlockSpec(head_gains.shape, lambda i, j: (0, 0)),
    ]
    args = [h2, gain.reshape(1, d), w, head_gains]
    if rope_half:
        tiles_per_seq = seq // tm
        for t in rope_tabs:
            in_specs.append(pl.BlockSpec((tm, LANES), lambda i, j: (i % tiles_per_seq, 0)))
            args.append(t)
    return pl.pallas_call(
        functools.partial(_proj_kernel, sections=sections, rope_half=rope_half),
        grid=(m // tm, n // tn),
        in_specs=in_specs,
        out_specs=pl.BlockSpec((tm, tn), lambda i, j: (i, j)),
        out_shape=jax.ShapeDtypeStruct((m, n), BF16),
        scratch_shapes=[pltpu.VMEM((tm, d), BF16)],
        compiler_params=_cparams(("parallel", "arbitrary")),
        name="proj",
    )(*args)


def _proj_idx_kernel(x_ref, g_ref, w_ref, kg_ref, ci_ref, sui_ref, sdi_ref, ct_ref, sut_ref, sdt_ref,
                     qi_ref, tail_ref, xn_ref, *, wi_scale):
    _norm_rows(x_ref, g_ref, xn_ref)
    acc = _dot(xn_ref[...], w_ref[...])
    half = IDX_ROPE_DIM // 2
    n_qi = qi_ref.shape[1]
    for c in range(n_qi // LANES):
        y = _rope(acc[:, c * LANES:(c + 1) * LANES], ci_ref[...], sui_ref[...], sdi_ref[...], half)
        qi_ref[:, c * LANES:(c + 1) * LANES] = y.astype(qi_ref.dtype)
    t = acc[:, n_qi:n_qi + LANES]
    is_k = lax.broadcasted_iota(jnp.int32, t.shape, 1) < IDX_DIM
    ms = jnp.sum(jnp.where(is_k, t * t, 0.0), axis=-1, keepdims=True) * (1.0 / IDX_DIM)
    y = jnp.where(is_k, t * lax.rsqrt(ms + RMS_EPS) * kg_ref[...], t * wi_scale)
    tail_ref[...] = _rope(y, ct_ref[...], sut_ref[...], sdt_ref[...], half)


def _proj_idx(h2, gain, w, kgain_pad, tabs, seq, tm=512):
    m, d = h2.shape
    n = w.shape[1]
    n_qi = IDX_HEADS * IDX_DIM
    tm = min(tm, m)
    tiles_per_seq = seq // tm
    const = lambda i: (0, 0)
    tab_spec = pl.BlockSpec((tm, LANES), lambda i: (i % tiles_per_seq, 0))
    return pl.pallas_call(
        functools.partial(_proj_idx_kernel, wi_scale=(IDX_HEADS ** -0.5) * (IDX_DIM ** -0.5)),
        grid=(m // tm,),
        in_specs=[pl.BlockSpec((tm, d), lambda i: (i, 0)), pl.BlockSpec((1, d), const),
                  pl.BlockSpec((d, n), const), pl.BlockSpec((1, LANES), const)] + [tab_spec] * 6,
        out_specs=[pl.BlockSpec((tm, n_qi), lambda i: (i, 0)), pl.BlockSpec((tm, LANES), lambda i: (i, 0))],
        out_shape=[jax.ShapeDtypeStruct((m, n_qi), BF16), jax.ShapeDtypeStruct((m, LANES), F32)],
        scratch_shapes=[pltpu.VMEM((tm, d), BF16)],
        compiler_params=_cparams(("parallel",)),
        name="proj_idx",
    )(h2, gain.reshape(1, d), w, kgain_pad, *tabs)


def _fgate_kernel(x_ref, g_ref, w_ref, b_ref, o_ref, xn_ref, carry_ref, *, tiles_per_seq):
    i = pl.program_id(0)
    _norm_rows(x_ref, g_ref, xn_ref)
    y = _dot(xn_ref[...], w_ref[...]) + b_ref[...]
    lf = (jnp.minimum(y, 0.0) - jnp.log1p(jnp.exp(-jnp.abs(y)))) * LOG2E
    tm = lf.shape[0]
    tri = (lax.broadcasted_iota(jnp.int32, (tm, tm), 0) >= lax.broadcasted_iota(jnp.int32, (tm, tm), 1))
    tri = jnp.where(tri, 1.0, 0.0).astype(BF16)
    a = lf.astype(BF16)
    r1 = lf - a.astype(F32)
    b = r1.astype(BF16)
    c = (r1 - b.astype(F32)).astype(BF16)
    cs = _dot(tri, a) + _dot(tri, b) + _dot(tri, c)

    @pl.when(i % tiles_per_seq == 0)
    def _():
        carry_ref[...] = jnp.zeros_like(carry_ref)

    out = cs + carry_ref[...]
    o_ref[...] = out
    carry_ref[...] = out[tm - 1:tm, :]


def _fgate_cumsum(h2, gain, w_pad, b_pad, seq, tm=512):
    m, d = h2.shape
    tm = min(tm, seq)
    const = lambda i: (0, 0)
    return pl.pallas_call(
        functools.partial(_fgate_kernel, tiles_per_seq=seq // tm),
        grid=(m // tm,),
        in_specs=[pl.BlockSpec((tm, d), lambda i: (i, 0)), pl.BlockSpec((1, d), const),
                  pl.BlockSpec((d, LANES), const), pl.BlockSpec((1, LANES), const)],
        out_specs=pl.BlockSpec((tm, LANES), lambda i: (i, 0)),
        out_shape=jax.ShapeDtypeStruct((m, LANES), F32),
        scratch_shapes=[pltpu.VMEM((tm, d), BF16), pltpu.VMEM((1, LANES), F32)],
        compiler_params=_cparams(("arbitrary",)),
        name="fgate_cumsum",
    )(h2, gain.reshape(1, d), w_pad, b_pad)


def _oproj_kernel(h_ref, mix_ref, w_ref, o_ref):
    o_ref[...] = h_ref[...] + _dot(mix_ref[...], w_ref[...])


def _oproj(h2, mix2, w, tm=512):
    m, d = h2.shape
    tm = min(tm, m)
    return pl.pallas_call(
        _oproj_kernel,
        grid=(m // tm,),
        in_specs=[pl.BlockSpec((tm, d), lambda i: (i, 0)), pl.BlockSpec((tm, d), lambda i: (i, 0)),
                  pl.BlockSpec((d, d), lambda i: (0, 0))],
        out_specs=pl.BlockSpec((tm, d), lambda i: (i, 0)),
        out_shape=jax.ShapeDtypeStruct((m, d), F32),
        compiler_params=_cparams(("parallel",)),
        name="oproj",
    )(h2, mix2, w)


def _mlp_kernel(x_ref, g_ref, wu_ref, wd_ref, o_ref, xn_ref):
    f = pl.program_id(1)

    @pl.when(f == 0)
    def _():
        _norm_rows(x_ref, g_ref, xn_ref)
        o_ref[...] = x_ref[...]

    a = jnp.maximum(_dot(xn_ref[...], wu_ref[...]), 0.0)
    o_ref[...] += _dot((a * a).astype(BF16), wd_ref[...])


def _mlp(h2, gain, w_up, w_down, tm=512, tf=512):
    m, d = h2.shape
    dff = w_up.shape[1]
    tm = min(tm, m)
    return pl.pallas_call(
        _mlp_kernel,
        grid=(m // tm, dff // tf),
        in_specs=[pl.BlockSpec((tm, d), lambda i, f: (i, 0)), pl.BlockSpec((1, d), lambda i, f: (0, 0)),
                  pl.BlockSpec((d, tf), lambda i, f: (0, f)), pl.BlockSpec((tf, d), lambda i, f: (f, 0))],
        out_specs=pl.BlockSpec((tm, d), lambda i, f: (i, 0)),
        out_shape=jax.ShapeDtypeStruct((m, d), F32),
        scratch_shapes=[pltpu.VMEM((tm, d), BF16)],
        compiler_params=_cparams(("parallel", "arbitrary")),
        name="mlp",
    )(h2, gain.reshape(1, d), w_up, w_down)


def _ple_kernel(x_ref, g_ref, wg_ref, p_ref, wp_ref, o_ref, xn_ref):
    _norm_rows(x_ref, g_ref, xn_ref)
    gate = jax.nn.sigmoid(_dot(xn_ref[...], wg_ref[...]))
    emb = _dot(p_ref[...].astype(BF16), wp_ref[...])
    o_ref[...] = x_ref[...] + gate * emb


def _ple(h2, gain, w_gate, p2, w_proj, tm=512):
    m, d = h2.shape
    pd = p2.shape[1]
    tm = min(tm, m)
    const = lambda i: (0, 0)
    return pl.pallas_call(
        _ple_kernel,
        grid=(m // tm,),
        in_specs=[pl.BlockSpec((tm, d), lambda i: (i, 0)), pl.BlockSpec((1, d), const),
                  pl.BlockSpec((d, d), const), pl.BlockSpec((tm, pd), lambda i: (i, 0)),
                  pl.BlockSpec((pd, d), const)],
        out_specs=pl.BlockSpec((tm, d), lambda i: (i, 0)),
        out_shape=jax.ShapeDtypeStruct((m, d), F32),
        scratch_shapes=[pltpu.VMEM((tm, d), BF16)],
        compiler_params=_cparams(("parallel",)),
        name="ple",
    )(h2, gain.reshape(1, d), w_gate, p2, w_proj)


def _sb_kernel(q_ref, k_ref, v_ref, o_ref, acc_ref, aft_ref, *, tq):
    i = pl.program_id(2)
    q = q_ref[0]
    row = lax.broadcasted_iota(jnp.int32, (tq, tq), 0)
    col = lax.broadcasted_iota(jnp.int32, (tq, tq), 1)
    later = jnp.where(row > col, 1.0, 0.0).astype(BF16)
    strict = col < row

    def block(j, diagonal):
        start = pl.multiple_of(j * tq, tq)
        kb = k_ref[0, pl.ds(start, tq), :]
        vb = v_ref[0, pl.ds(start, tq), :]
        z = _nt_dot(q, kb)
        sp = jnp.maximum(z, 0.0) + jnp.log1p(jnp.exp(-jnp.abs(z)))
        log_1mb = -sp
        if diagonal:
            log_1mb = jnp.where(strict, log_1mb, 0.0)
        hi = log_1mb.astype(BF16)
        lo = (log_1mb - hi.astype(F32)).astype(BF16)
        log_w = (z - sp) + _dot(hi, later) + _dot(lo, later)
        return log_w, jnp.sum(log_1mb, axis=-1, keepdims=True), vb

    log_w0, sum0, v0 = block(i, True)
    log_w1, sum1, v1 = block(jnp.maximum(i - 1, 0), False)
    has_prev = (i > 0).astype(F32)
    w0 = jnp.where(strict, jnp.exp(log_w0), 0.0)
    w1 = jnp.exp(log_w1 + sum0) * has_prev
    acc_ref[...] = _dot(w0.astype(BF16), v0) + _dot(w1.astype(BF16), v1)
    aft_ref[...] = sum0 + sum1 * has_prev

    def cond(c):
        j, worst = c
        return jnp.logical_and(j >= 0, worst > EXP_UNDERFLOW)

    def body(c):
        j, _ = c
        log_w, row_sum, vb = block(j, False)
        acc_ref[...] += _dot(jnp.exp(log_w + aft_ref[...]).astype(BF16), vb)
        aft_ref[...] += row_sum
        return j - 1, jnp.max(aft_ref[...])

    lax.while_loop(cond, body, (i - 2, jnp.max(aft_ref[...])))
    o_ref[0] = acc_ref[...].astype(o_ref.dtype)


def _sb_attention(qkv, tq=256):
    bsz, seq, _ = qkv.shape
    tq = min(tq, seq)
    return pl.pallas_call(
        functools.partial(_sb_kernel, tq=tq),
        grid=(bsz, N_HEADS, seq // tq),
        in_specs=[pl.BlockSpec((1, tq, HEAD_DIM), lambda b, h, i: (b, i, h)),
                  pl.BlockSpec((1, seq, HEAD_DIM), lambda b, h, i: (b, 0, N_HEADS + h)),
                  pl.BlockSpec((1, seq, HEAD_DIM), lambda b, h, i: (b, 0, 2 * N_HEADS + h))],
        out_specs=pl.BlockSpec((1, tq, HEAD_DIM), lambda b, h, i: (b, i, h)),
        out_shape=jax.ShapeDtypeStruct((bsz, seq, N_HEADS * HEAD_DIM), BF16),
        scratch_shapes=[pltpu.VMEM((tq, HEAD_DIM), F32), pltpu.VMEM((tq, 1), F32)],
        compiler_params=_cparams(("parallel", "parallel", "arbitrary")),
        name="sb_attn",
    )(qkv, qkv, qkv)


def _tile_lanes(x, n):
    return x if n == 1 else jnp.concatenate([x] * n, axis=1)


def _fox_kernel(q_ref, k_ref, vt_ref, cum_ref, cumt_ref, o_ref, acc_ref, m_ref, csb_ref, *, t):
    h = pl.program_id(1)
    i = pl.program_id(2)
    seq = k_ref.shape[1]
    n_sub = t // LANES

    @pl.when(i == 0)
    def _():
        lane = lax.broadcasted_iota(jnp.int32, (t, LANES), 1)

        def fill(c, carry):
            st = pl.multiple_of(c * t, t)
            col = jnp.sum(jnp.where(lane == h, cum_ref[0, pl.ds(st, t), :], 0.0), axis=-1, keepdims=True)
            csb_ref[pl.ds(st, t), :] = jnp.broadcast_to(col, (t, LANES))
            return carry

        lax.fori_loop(0, seq // t, fill, 0)

    q = q_ref[0]
    ct = cumt_ref[0, 0, :, pl.ds(pl.multiple_of(i * t, t), t)]
    causal = (lax.broadcasted_iota(jnp.int32, (t, t), 0) <= lax.broadcasted_iota(jnp.int32, (t, t), 1))

    def scores(j, diagonal=False):
        st = pl.multiple_of(j * t, t)
        zt = _nt_dot(k_ref[0, pl.ds(st, t), :], q) + ct - _tile_lanes(csb_ref[pl.ds(st, t), :], n_sub)
        return jnp.where(causal, zt, NEG_BIG) if diagonal else zt

    def update(j, zt):
        vtb = vt_ref[0, 0, :, pl.ds(pl.multiple_of(j * t, t), t)]
        m_old = m_ref[...]
        m_new = jnp.maximum(m_old, jnp.max(zt, axis=0, keepdims=True))
        p = jnp.exp2(zt - m_new).astype(BF16)
        acc_ref[...] = jnp.exp2(m_old - m_new) * acc_ref[...] + _dot(vtb, p)
        m_ref[...] = m_new

    m_ref[...] = jnp.full_like(m_ref, NEG_BIG)
    acc_ref[...] = jnp.zeros_like(acc_ref)

    def pair(jj, carry):
        za = scores(2 * jj)
        zb = scores(2 * jj + 1)
        update(2 * jj, za)
        update(2 * jj + 1, zb)
        return carry

    lax.fori_loop(0, i // 2, pair, 0)

    @pl.when(i % 2 == 1)
    def _():
        update(i - 1, scores(i - 1))

    update(i, scores(i, True))
    o_ref[0] = jnp.transpose(acc_ref[0:HEAD_DIM, :] / acc_ref[HEAD_DIM:HEAD_DIM + 1, :]).astype(o_ref.dtype)


def _fox_attention(qkv, v_t, cum, cum_t, t=512):
    bsz, seq, _ = qkv.shape
    t = min(t, seq)
    dv = HEAD_DIM + ONES_ROWS
    return pl.pallas_call(
        functools.partial(_fox_kernel, t=t),
        grid=(bsz, N_HEADS, seq // t),
        in_specs=[pl.BlockSpec((1, t, HEAD_DIM), lambda b, h, i: (b, i, h)),
                  pl.BlockSpec((1, seq, HEAD_DIM), lambda b, h, i: (b, 0, N_HEADS + h)),
                  pl.BlockSpec((1, 1, dv, seq), lambda b, h, i: (b, h, 0, 0)),
                  pl.BlockSpec((1, seq, LANES), lambda b, h, i: (b, 0, 0)),
                  pl.BlockSpec((1, 1, 1, seq), lambda b, h, i: (b, h, 0, 0))],
        out_specs=pl.BlockSpec((1, t, HEAD_DIM), lambda b, h, i: (b, i, h)),
        out_shape=jax.ShapeDtypeStruct((bsz, seq, N_HEADS * HEAD_DIM), BF16),
        scratch_shapes=[pltpu.VMEM((dv, t), F32), pltpu.VMEM((1, t), F32), pltpu.VMEM((seq, LANES), F32)],
        compiler_params=_cparams(("parallel", "parallel", "arbitrary")),
        name="fox_attn",
    )(qkv, qkv, v_t, cum, cum_t)


def _sortable(x):
    bits = pltpu.bitcast(x, jnp.int32)
    return jnp.where(bits < 0, bits ^ jnp.int32(0x7FFFFFFF), bits)


def _dsa_kernel(q_ref, k_ref, vt_ref, qi_ref, ki_ref, wi_ref, o_ref,
                key_ref, q4_ref, thr_ref, acc_ref, m_ref, *, t, k_top, count_rows):
    i = pl.program_id(1)
    n_blocks = i + 1
    group = N_HEADS // B_KV_HEADS
    qs = i * t

    for g in range(B_KV_HEADS):
        q4_ref[g] = jnp.concatenate(
            [q_ref[0, :, (g * group + r) * HEAD_DIM:(g * group + r + 1) * HEAD_DIM] for r in range(group)], axis=0)

    key_pos = lax.broadcasted_iota(jnp.int32, (t, t), 0)
    chunk_end = ((qs + lax.broadcasted_iota(jnp.int32, (1, t), 1)) // CHUNK + 1) * CHUNK

    def score_block(j, carry):
        st = pl.multiple_of(j * t, t)
        ka = ki_ref[0, pl.ds(st, t), 0:LANES]
        kb = ki_ref[0, pl.ds(st, t), LANES:2 * LANES]
        score = jnp.zeros((t, t), F32)
        for mblk in range(IDX_HEADS // 2):
            qm = qi_ref[0, :, mblk * LANES:(mblk + 1) * LANES]
            for half, kk in enumerate((ka, kb)):
                hd = 2 * mblk + half
                score = score + jnp.maximum(_nt_dot(kk, qm), 0.0) * wi_ref[0, hd:hd + 1, :]
        admissible = (st + key_pos) < chunk_end
        key_ref[pl.ds(st, t), :] = jnp.where(admissible, _sortable(score), KEY_NEG_INF)
        return carry

    lax.fori_loop(0, n_blocks, score_block, 0)

    n_chunks = n_blocks * (t // count_rows)

    def bit_body(it, lo):
        cand = lo + jnp.left_shift(jnp.int32(1), 31 - it)

        def count_body(c, cnt):
            off = pl.multiple_of(c * count_rows, count_rows)
            return cnt + jnp.where(key_ref[pl.ds(off, count_rows), :] >= cand, 1.0, 0.0)

        cnt = lax.fori_loop(0, n_chunks, count_body, jnp.zeros((count_rows, t), F32))
        total = jnp.sum(cnt, axis=0, keepdims=True)
        return jnp.where(total >= k_top, cand, lo)

    lo = lax.fori_loop(0, 32, bit_body, jnp.full((1, t), INT_MIN, jnp.int32))
    thr_ref[...] = jnp.maximum(lo, KEY_NEG_INF + 1)

    m_ref[...] = jnp.full_like(m_ref, NEG_BIG)
    acc_ref[...] = jnp.zeros_like(acc_ref)

    def attn_block(j, carry):
        st = pl.multiple_of(j * t, t)
        bias = jnp.where(key_ref[pl.ds(st, t), :] >= thr_ref[...], 0.0, NEG_BIG)
        bias4 = _tile_lanes(bias, group)
        scores = [_nt_dot(k_ref[0, pl.ds(st, t), g * HEAD_DIM:(g + 1) * HEAD_DIM], q4_ref[g]) + bias4
                  for g in range(B_KV_HEADS)]
        for g in range(B_KV_HEADS):
            vtb = vt_ref[0, g, :, pl.ds(st, t)]
            zt = scores[g]
            m_old = m_ref[g]
            m_new = jnp.maximum(m_old, jnp.max(zt, axis=0, keepdims=True))
            p = jnp.exp2(zt - m_new).astype(BF16)
            acc_ref[g] = jnp.exp2(m_old - m_new) * acc_ref[g] + _dot(vtb, p)
            m_ref[g] = m_new
        return carry

    lax.fori_loop(0, n_blocks, attn_block, 0)
    for g in range(B_KV_HEADS):
        out_t = acc_ref[g, 0:HEAD_DIM, :] / acc_ref[g, HEAD_DIM:HEAD_DIM + 1, :]
        for r in range(group):
            hd = g * group + r
            o_ref[0, :, hd * HEAD_DIM:(hd + 1) * HEAD_DIM] = jnp.transpose(
                out_t[:, r * t:(r + 1) * t]).astype(o_ref.dtype)


def _dsa_attention(qkv, v_t, qi, ki2, wi_t, k_top, t=256, count_rows=128):
    bsz, seq, _ = qkv.shape
    t = min(t, seq)
    count_rows = min(count_rows, t)
    dq = N_HEADS * HEAD_DIM
    dkv = B_KV_HEADS * HEAD_DIM
    dv = HEAD_DIM + ONES_ROWS
    group = N_HEADS // B_KV_HEADS
    resident = dict(pipeline_mode=pl.Buffered(1))
    return pl.pallas_call(
        functools.partial(_dsa_kernel, t=t, k_top=k_top, count_rows=count_rows),
        grid=(bsz, seq // t),
        in_specs=[pl.BlockSpec((1, t, dq), lambda b, i: (b, i, 0)),
                  pl.BlockSpec((1, seq, dkv), lambda b, i: (b, 0, dq // dkv), **resident),
                  pl.BlockSpec((1, B_KV_HEADS, dv, seq), lambda b, i: (b, 0, 0, 0), **resident),
                  pl.BlockSpec((1, t, IDX_HEADS * IDX_DIM), lambda b, i: (b, i, 0)),
                  pl.BlockSpec((1, seq, 2 * LANES), lambda b, i: (b, 0, 0), **resident),
                  pl.BlockSpec((1, IDX_HEADS, t), lambda b, i: (b, 0, i))],
        out_specs=pl.BlockSpec((1, t, dq), lambda b, i: (b, i, 0)),
        out_shape=jax.ShapeDtypeStruct((bsz, seq, dq), BF16),
        scratch_shapes=[pltpu.VMEM((seq, t), jnp.int32),
                        pltpu.VMEM((B_KV_HEADS, group * t, HEAD_DIM), BF16),
                        pltpu.VMEM((1, t), jnp.int32),
                        pltpu.VMEM((B_KV_HEADS, dv, group * t), F32),
                        pltpu.VMEM((B_KV_HEADS, 1, group * t), F32)],
        compiler_params=_cparams(("parallel", "arbitrary")),
        name="dsa_attn",
    )(qkv, qkv, v_t, qi, ki2, wi_t)


def _rope_tables(seq, rot_dim, period, active_lanes):
    half = rot_dim // 2
    inv = ROPE_THETA ** (-jnp.arange(half, dtype=F32) / half)
    ang = jnp.arange(seq, dtype=F32)[:, None] * inv[None, :]
    cos, sin = jnp.cos(ang), jnp.sin(ang)
    lane = jnp.arange(LANES)
    local = lane % period
    idx = local % half
    first = (local < half) & (lane < active_lanes)
    second = (local >= half) & (local < rot_dim) & (lane < active_lanes)
    c = jnp.where((first | second)[None, :], cos[:, idx], 1.0)
    s_up = jnp.where(first[None, :], -sin[:, idx], 0.0)
    s_dn = jnp.where(second[None, :], sin[:, idx], 0.0)
    return c, s_up, s_dn


def _transpose_v(v, n_heads):
    bsz, seq, _ = v.shape
    v_t = jnp.transpose(v.reshape(bsz, seq, n_heads, HEAD_DIM), (0, 2, 3, 1))
    return jnp.concatenate([v_t, jnp.ones((bsz, n_heads, ONES_ROWS, seq), v.dtype)], axis=2)


def _pad_lanes(v, fill=0.0):
    return jnp.pad(v.astype(F32), (0, LANES - v.shape[0]), constant_values=fill).reshape(1, LANES)


def kernel(x, p, attn_norm, w_o, mlp_norm, w_up, w_down, ple_norm, w_ple_gate, w_ple_proj,
           a_w_in, a_q_norm, a_k_norm, b_w_in, b_q_norm, b_k_norm, b_idx_k_norm,
           c_w_in, c_b_f, c_q_norm, c_k_norm):
    bsz, seq, d = x.shape
    depth = attn_norm.shape[0]
    m = bsz * seq
    dm = N_HEADS * HEAD_DIM
    dkv = B_KV_HEADS * HEAD_DIM
    n_qi = IDX_HEADS * IDX_DIM
    scale = HEAD_DIM ** -0.5
    k_top = min(TOPK_MAX, seq // 4)

    h = x.reshape(m, d)
    for layer in range(depth):
        kind, j = layer % N_MIXERS, layer // N_MIXERS
        if kind == 0:
            gains = jnp.stack([a_q_norm[j] * scale, a_k_norm[j]])
            nt = dm // 1024
            qkv = _proj(h, attn_norm[layer], a_w_in[j].astype(BF16), gains,
                        ((0, nt, 0, False), (nt, 2 * nt, 1, False), (2 * nt, 3 * nt, None, False)), 1024)
            mix = _sb_attention(qkv.reshape(bsz, seq, 3 * dm))
        elif kind == 1:
            w = b_w_in[j]
            gains = jnp.stack([b_q_norm[j] * (scale * LOG2E), b_k_norm[j]])
            main_tabs = _rope_tables(seq, ROPE_DIM, HEAD_DIM, LANES)
            tn = dkv
            qkv = _proj(h, attn_norm[layer], w[:, :dm + 2 * dkv].astype(BF16), gains,
                        ((0, dm // tn, 0, True), (dm // tn, dm // tn + 1, 1, True),
                         (dm // tn + 1, dm // tn + 2, None, False)),
                        tn, rope_tabs=main_tabs, rope_half=ROPE_DIM // 2, seq=seq)
            tail_w = IDX_DIM + IDX_HEADS
            w_idx = jnp.pad(w[:, dm + 2 * dkv:], ((0, 0), (0, LANES - tail_w))).astype(BF16)
            idx_tabs = _rope_tables(seq, IDX_ROPE_DIM, IDX_DIM, LANES)
            tail_tabs = _rope_tables(seq, IDX_ROPE_DIM, IDX_DIM, IDX_DIM)
            qi, tail = _proj_idx(h, attn_norm[layer], w_idx, _pad_lanes(b_idx_k_norm[j]),
                                 idx_tabs + tail_tabs, seq)
            ki = tail[:, :IDX_DIM].astype(BF16)
            zeros = jnp.zeros((m, LANES - IDX_DIM), BF16)
            ki2 = jnp.concatenate([ki, zeros, zeros, ki], axis=1).reshape(bsz, seq, 2 * LANES)
            wi_t = jnp.transpose(tail[:, IDX_DIM:tail_w].reshape(bsz, seq, IDX_HEADS), (0, 2, 1))
            qkv = qkv.reshape(bsz, seq, dm + 2 * dkv)
            v_t = _transpose_v(qkv[:, :, dm + dkv:], B_KV_HEADS)
            mix = _dsa_attention(qkv, v_t, qi.reshape(bsz, seq, n_qi), ki2, wi_t, k_top)
        else:
            w = c_w_in[j]
            gains = jnp.stack([c_q_norm[j] * (scale * LOG2E), c_k_norm[j]])
            nt = dm // 1024
            qkv = _proj(h, attn_norm[layer], w[:, :3 * dm].astype(BF16), gains,
                        ((0, nt, 0, False), (nt, 2 * nt, 1, False), (2 * nt, 3 * nt, None, False)), 1024)
            w_f = jnp.pad(w[:, 3 * dm:], ((0, 0), (0, LANES - N_HEADS))).astype(BF16)
            cum = _fgate_cumsum(h, attn_norm[layer], w_f, _pad_lanes(c_b_f[j]), seq)
            cum3 = cum.reshape(bsz, seq, LANES)
            cum_t = jnp.transpose(cum3[:, :, :N_HEADS], (0, 2, 1)).reshape(bsz, N_HEADS, 1, seq)
            qkv = qkv.reshape(bsz, seq, 3 * dm)
            v_t = _transpose_v(qkv[:, :, 2 * dm:], N_HEADS)
            mix = _fox_attention(qkv, v_t, cum3, cum_t)
        h = _oproj(h, mix.reshape(m, dm), w_o[layer].astype(BF16))
        h = _mlp(h, mlp_norm[layer], w_up[layer].astype(BF16), w_down[layer].astype(BF16))
        h = _ple(h, ple_norm[layer], w_ple_gate[layer].astype(BF16), p[layer].reshape(m, -1),
                 w_ple_proj[layer].astype(BF16))
    return h.reshape(bsz, seq, d)
```

```python
import functools
import math

import jax
import jax.numpy as jnp
from jax import lax
from jax.experimental import pallas as pl
from jax.experimental.pallas import tpu as pltpu

F32 = jnp.float32
BF16 = jnp.bfloat16

N_HEADS = 16
HEAD_DIM = 128
ROPE_DIM = HEAD_DIM // 4
ROPE_THETA = 500000.0
B_KV_HEADS = 4
IDX_HEADS = 16
IDX_DIM = 64
IDX_ROPE_DIM = IDX_DIM // 4
TOPK_MAX = 256
CHUNK = 64
RMS_EPS = 1e-6
N_MIXERS = 3

LANES = 128
VMEM_LIMIT = 56 * 1024 * 1024
NEG_BIG = -1e30
EXP_UNDERFLOW = -104.0
LOG2E = math.log2(math.e)
ONES_ROWS = 16
INT_MIN = -(2 ** 31)
KEY_NEG_INF = INT_MIN + 0x7FFFFF


def _cparams(sem):
    return pltpu.CompilerParams(dimension_semantics=sem, vmem_limit_bytes=VMEM_LIMIT)


def _nt_dot(a, b):
    return lax.dot_general(a, b, (((1,), (1,)), ((), ())), preferred_element_type=F32)


def _dot(a, b):
    return jnp.dot(a, b, preferred_element_type=F32)


def _norm_rows(x_ref, g_ref, xn_ref, rows=256):
    tm = x_ref.shape[0]
    rows = min(rows, tm)
    for r in range(0, tm, rows):
        x = x_ref[r:r + rows, :]
        ms = jnp.mean(x * x, axis=-1, keepdims=True)
        xn_ref[r:r + rows, :] = (x * lax.rsqrt(ms + RMS_EPS) * g_ref[...]).astype(BF16)


def _head_norm(blk, gain_row):
    ms = jnp.mean(blk * blk, axis=-1, keepdims=True)
    return blk * lax.rsqrt(ms + RMS_EPS) * gain_row


def _rope(y, c, s_up, s_dn, half):
    return y * c + pltpu.roll(y, LANES - half, 1) * s_up + pltpu.roll(y, half, 1) * s_dn


def _proj_kernel(*refs, q_tiles, k_tiles, rope_half):
    if rope_half:
        x_ref, g_ref, w_ref, hg_ref, c_ref, s_ref, perm_ref, o_ref, xn_ref = refs
    else:
        x_ref, g_ref, w_ref, hg_ref, o_ref, xn_ref = refs
    j = pl.program_id(1)

    @pl.when(j == 0)
    def _():
        _norm_rows(x_ref, g_ref, xn_ref)

    acc = _dot(xn_ref[...], w_ref[...])
    tn = acc.shape[1]
    normed = j < q_tiles + k_tiles
    gain = hg_ref[pl.ds((j >= q_tiles).astype(jnp.int32), 1), :]
    heads = 2 if rope_half else 1
    width = heads * LANES
    for c in range(tn // width):
        blk = acc[:, c * width:(c + 1) * width]
        y = jnp.concatenate([_head_norm(blk[:, k * LANES:(k + 1) * LANES], gain) for k in range(heads)], axis=1)
        if rope_half:
            partner = _dot(y.astype(BF16), perm_ref[...])
            y = y * _tile_lanes(c_ref[...], heads) + partner * _tile_lanes(s_ref[...], heads)
        o_ref[:, c * width:(c + 1) * width] = jnp.where(normed, y, blk).astype(o_ref.dtype)


def _proj(h2, gain, w, head_gains, q_tiles, k_tiles, tn, rope_tabs=None, rope_half=0, seq=None, tm=1024):
    m, d = h2.shape
    n = w.shape[1]
    tm = min(tm, seq) if rope_half else min(tm, m)
    in_specs = [
        pl.BlockSpec((tm, d), lambda i, j: (i, 0)),
        pl.BlockSpec((1, d), lambda i, j: (0, 0)),
        pl.BlockSpec((d, tn), lambda i, j: (0, j)),
        pl.BlockSpec(head_gains.shape, lambda i, j: (0, 0)),
    ]
    args = [h2, gain.reshape(1, d), w, head_gains]
    if rope_half:
        tiles_per_seq = seq // tm
        cos_tab, sin_tab, perm = rope_tabs
        for t in (cos_tab, sin_tab):
            in_specs.append(pl.BlockSpec((tm, LANES), lambda i, j: (i % tiles_per_seq, 0)))
            args.append(t)
        in_specs.append(pl.BlockSpec(perm.shape, lambda i, j: (0, 0)))
        args.append(perm)
    return pl.pallas_call(
        functools.partial(_proj_kernel, q_tiles=q_tiles, k_tiles=k_tiles, rope_half=rope_half),
        grid=(m // tm, n // tn),
        in_specs=in_specs,
        out_specs=pl.BlockSpec((tm, tn), lambda i, j: (i, j)),
        out_shape=jax.ShapeDtypeStruct((m, n), BF16),
        scratch_shapes=[pltpu.VMEM((tm, d), BF16)],
        compiler_params=_cparams(("parallel", "arbitrary")),
        name="proj",
    )(*args)


def _proj_idx_kernel(x_ref, g_ref, w_ref, kg_ref, ci_ref, sui_ref, sdi_ref, ct_ref, sut_ref, sdt_ref,
                     qi_ref, tail_ref, xn_ref, *, wi_scale):
    _norm_rows(x_ref, g_ref, xn_ref)
    acc = _dot(xn_ref[...], w_ref[...])
    half = IDX_ROPE_DIM // 2
    n_qi = qi_ref.shape[1]
    for c in range(n_qi // LANES):
        y = _rope(acc[:, c * LANES:(c + 1) * LANES], ci_ref[...], sui_ref[...], sdi_ref[...], half)
        qi_ref[:, c * LANES:(c + 1) * LANES] = y.astype(qi_ref.dtype)
    t = acc[:, n_qi:n_qi + LANES]
    is_k = lax.broadcasted_iota(jnp.int32, t.shape, 1) < IDX_DIM
    ms = jnp.sum(jnp.where(is_k, t * t, 0.0), axis=-1, keepdims=True) * (1.0 / IDX_DIM)
    y = jnp.where(is_k, t * lax.rsqrt(ms + RMS_EPS) * kg_ref[...], t * wi_scale)
    tail_ref[...] = _rope(y, ct_ref[...], sut_ref[...], sdt_ref[...], half)


def _proj_idx(h2, gain, w, kgain_pad, tabs, seq, tm=512):
    m, d = h2.shape
    n = w.shape[1]
    n_qi = IDX_HEADS * IDX_DIM
    tm = min(tm, seq)
    tiles_per_seq = seq // tm
    const = lambda i: (0, 0)
    tab_spec = pl.BlockSpec((tm, LANES), lambda i: (i % tiles_per_seq, 0))
    return pl.pallas_call(
        functools.partial(_proj_idx_kernel, wi_scale=(IDX_HEADS ** -0.5) * (IDX_DIM ** -0.5)),
        grid=(m // tm,),
        in_specs=[pl.BlockSpec((tm, d), lambda i: (i, 0)), pl.BlockSpec((1, d), const),
                  pl.BlockSpec((d, n), const), pl.BlockSpec((1, LANES), const)] + [tab_spec] * 6,
        out_specs=[pl.BlockSpec((tm, n_qi), lambda i: (i, 0)), pl.BlockSpec((tm, LANES), lambda i: (i, 0))],
        out_shape=[jax.ShapeDtypeStruct((m, n_qi), BF16), jax.ShapeDtypeStruct((m, LANES), F32)],
        scratch_shapes=[pltpu.VMEM((tm, d), BF16)],
        compiler_params=_cparams(("parallel",)),
        name="proj_idx",
    )(h2, gain.reshape(1, d), w, kgain_pad, *tabs)


def _fgate_kernel(x_ref, g_ref, w_ref, b_ref, o_ref, xn_ref, carry_ref, *, tiles_per_seq):
    i = pl.program_id(0)
    _norm_rows(x_ref, g_ref, xn_ref)
    y = _dot(xn_ref[...], w_ref[...]) + b_ref[...]
    lf = (jnp.minimum(y, 0.0) - jnp.log1p(jnp.exp(-jnp.abs(y)))) * LOG2E
    tm = lf.shape[0]
    tri = (lax.broadcasted_iota(jnp.int32, (tm, tm), 0) >= lax.broadcasted_iota(jnp.int32, (tm, tm), 1))
    tri = jnp.where(tri, 1.0, 0.0).astype(BF16)
    a = lf.astype(BF16)
    r1 = lf - a.astype(F32)
    b = r1.astype(BF16)
    c = (r1 - b.astype(F32)).astype(BF16)
    cs = _dot(tri, a) + _dot(tri, b) + _dot(tri, c)

    @pl.when(i % tiles_per_seq == 0)
    def _():
        carry_ref[...] = jnp.zeros_like(carry_ref)

    out = cs + carry_ref[...]
    o_ref[...] = out
    carry_ref[...] = out[tm - 1:tm, :]


def _fgate_cumsum(h2, gain, w_pad, b_pad, seq, tm=512):
    m, d = h2.shape
    tm = min(tm, seq)
    const = lambda i: (0, 0)
    return pl.pallas_call(
        functools.partial(_fgate_kernel, tiles_per_seq=seq // tm),
        grid=(m // tm,),
        in_specs=[pl.BlockSpec((tm, d), lambda i: (i, 0)), pl.BlockSpec((1, d), const),
                  pl.BlockSpec((d, LANES), const), pl.BlockSpec((1, LANES), const)],
        out_specs=pl.BlockSpec((tm, LANES), lambda i: (i, 0)),
        out_shape=jax.ShapeDtypeStruct((m, LANES), F32),
        scratch_shapes=[pltpu.VMEM((tm, d), BF16), pltpu.VMEM((1, LANES), F32)],
        compiler_params=_cparams(("arbitrary",)),
        name="fgate_cumsum",
    )(h2, gain.reshape(1, d), w_pad, b_pad)


def _oproj_kernel(h_ref, mix_ref, w_ref, o_ref):
    o_ref[...] = h_ref[...] + _dot(mix_ref[...], w_ref[...])


def _oproj(h2, mix2, w, tm=512):
    m, d = h2.shape
    tm = min(tm, m)
    return pl.pallas_call(
        _oproj_kernel,
        grid=(m // tm,),
        in_specs=[pl.BlockSpec((tm, d), lambda i: (i, 0)), pl.BlockSpec((tm, d), lambda i: (i, 0)),
                  pl.BlockSpec((d, d), lambda i: (0, 0))],
        out_specs=pl.BlockSpec((tm, d), lambda i: (i, 0)),
        out_shape=jax.ShapeDtypeStruct((m, d), F32),
        compiler_params=_cparams(("parallel",)),
        name="oproj",
    )(h2, mix2, w)


def _mlp_kernel(x_ref, g_ref, wu_ref, wd_ref, o_ref, xn_ref):
    f = pl.program_id(1)

    @pl.when(f == 0)
    def _():
        _norm_rows(x_ref, g_ref, xn_ref)
        o_ref[...] = x_ref[...]

    a = jnp.maximum(_dot(xn_ref[...], wu_ref[...]), 0.0)
    o_ref[...] += _dot((a * a).astype(BF16), wd_ref[...])


def _mlp(h2, gain, w_up, w_down, tm=512, tf=1024):
    m, d = h2.shape
    dff = w_up.shape[1]
    tm = min(tm, m)
    return pl.pallas_call(
        _mlp_kernel,
        grid=(m // tm, dff // tf),
        in_specs=[pl.BlockSpec((tm, d), lambda i, f: (i, 0)), pl.BlockSpec((1, d), lambda i, f: (0, 0)),
                  pl.BlockSpec((d, tf), lambda i, f: (0, f)), pl.BlockSpec((tf, d), lambda i, f: (f, 0))],
        out_specs=pl.BlockSpec((tm, d), lambda i, f: (i, 0)),
        out_shape=jax.ShapeDtypeStruct((m, d), F32),
        scratch_shapes=[pltpu.VMEM((tm, d), BF16)],
        compiler_params=_cparams(("parallel", "arbitrary")),
        name="mlp",
    )(h2, gain.reshape(1, d), w_up, w_down)


def _ple_kernel(x_ref, g_ref, wg_ref, p_ref, wp_ref, o_ref, xn_ref):
    _norm_rows(x_ref, g_ref, xn_ref)
    gate = jax.nn.sigmoid(_dot(xn_ref[...], wg_ref[...]))
    emb = _dot(p_ref[...].astype(BF16), wp_ref[...])
    o_ref[...] = x_ref[...] + gate * emb


def _ple(h2, gain, w_gate, p2, w_proj, tm=512):
    m, d = h2.shape
    pd = p2.shape[1]
    tm = min(tm, m)
    const = lambda i: (0, 0)
    return pl.pallas_call(
        _ple_kernel,
        grid=(m // tm,),
        in_specs=[pl.BlockSpec((tm, d), lambda i: (i, 0)), pl.BlockSpec((1, d), const),
                  pl.BlockSpec((d, d), const), pl.BlockSpec((tm, pd), lambda i: (i, 0)),
                  pl.BlockSpec((pd, d), const)],
        out_specs=pl.BlockSpec((tm, d), lambda i: (i, 0)),
        out_shape=jax.ShapeDtypeStruct((m, d), F32),
        scratch_shapes=[pltpu.VMEM((tm, d), BF16)],
        compiler_params=_cparams(("parallel",)),
        name="ple",
    )(h2, gain.reshape(1, d), w_gate, p2, w_proj)


def _sb_kernel(q_ref, k_ref, v_ref, o_ref, acc_ref, aft_ref, *, tq):
    i = pl.program_id(2)
    q = q_ref[0]
    row = lax.broadcasted_iota(jnp.int32, (tq, tq), 0)
    col = lax.broadcasted_iota(jnp.int32, (tq, tq), 1)
    later = jnp.where(row > col, 1.0, 0.0).astype(BF16)
    strict = col < row

    def block(j, diagonal):
        start = pl.multiple_of(j * tq, tq)
        kb = k_ref[0, pl.ds(start, tq), :]
        vb = v_ref[0, pl.ds(start, tq), :]
        z = _nt_dot(q, kb)
        sp = jnp.maximum(z, 0.0) + jnp.log(1.0 + jnp.exp(-jnp.abs(z)))
        log_1mb = -sp
        if diagonal:
            log_1mb = jnp.where(strict, log_1mb, 0.0)
        hi = log_1mb.astype(BF16)
        lo = (log_1mb - hi.astype(F32)).astype(BF16)
        log_w = (z - sp) + _dot(hi, later) + _dot(lo, later)
        return log_w, jnp.sum(log_1mb, axis=-1, keepdims=True), vb

    log_w0, sum0, v0 = block(i, True)
    log_w1, sum1, v1 = block(jnp.maximum(i - 1, 0), False)
    has_prev = (i > 0).astype(F32)
    w0 = jnp.where(strict, jnp.exp(log_w0), 0.0)
    w1 = jnp.exp(log_w1 + sum0) * has_prev
    acc_ref[...] = _dot(w0.astype(BF16), v0) + _dot(w1.astype(BF16), v1)
    aft_ref[...] = sum0 + sum1 * has_prev

    def cond(c):
        j, worst = c
        return jnp.logical_and(j >= 0, worst > EXP_UNDERFLOW)

    def body(c):
        j, _ = c
        log_w, row_sum, vb = block(j, False)
        acc_ref[...] += _dot(jnp.exp(log_w + aft_ref[...]).astype(BF16), vb)
        aft_ref[...] += row_sum
        return j - 1, jnp.max(aft_ref[...])

    lax.while_loop(cond, body, (i - 2, jnp.max(aft_ref[...])))
    o_ref[0] = acc_ref[...].astype(o_ref.dtype)


def _sb_attention(qkv, tq=256):
    bsz, seq, _ = qkv.shape
    tq = min(tq, seq)
    return pl.pallas_call(
        functools.partial(_sb_kernel, tq=tq),
        grid=(bsz, N_HEADS, seq // tq),
        in_specs=[pl.BlockSpec((1, tq, HEAD_DIM), lambda b, h, i: (b, i, h)),
                  pl.BlockSpec((1, seq, HEAD_DIM), lambda b, h, i: (b, 0, N_HEADS + h)),
                  pl.BlockSpec((1, seq, HEAD_DIM), lambda b, h, i: (b, 0, 2 * N_HEADS + h))],
        out_specs=pl.BlockSpec((1, tq, HEAD_DIM), lambda b, h, i: (b, i, h)),
        out_shape=jax.ShapeDtypeStruct((bsz, seq, N_HEADS * HEAD_DIM), BF16),
        scratch_shapes=[pltpu.VMEM((tq, HEAD_DIM), F32), pltpu.VMEM((tq, 1), F32)],
        compiler_params=_cparams(("parallel", "parallel", "arbitrary")),
        name="sb_attn",
    )(qkv, qkv, qkv)


def _tile_lanes(x, n):
    return x if n == 1 else jnp.concatenate([x] * n, axis=1)


def _fox_first_block_kernel(c0_ref, ce_ref, bound_ref, o_ref):
    c0 = c0_ref[...]
    n_blocks = ce_ref.shape[1]
    first = jnp.full(c0.shape, n_blocks, jnp.int32)
    for j in range(n_blocks):
        needed = c0 - ce_ref[:, j:j + 1] >= -bound_ref[...]
        first = jnp.minimum(first, jnp.where(needed, j, n_blocks))
    o_ref[...] = first


def _fox_first_block(cum_t, t, qk_bound):
    bsz, n_heads, _, seq = cum_t.shape
    rows = bsz * n_heads
    c0 = cum_t[:, :, 0, ::t].reshape(rows, seq // t)
    ce = cum_t[:, :, 0, t - 1::t].reshape(rows, seq // t)
    bound = (151.0 + 2.04 * qk_bound).astype(F32).reshape(1, 1)
    full = lambda shape: pl.BlockSpec(shape, lambda: (0,) * len(shape))
    return pl.pallas_call(
        _fox_first_block_kernel,
        in_specs=[full(c0.shape), full(ce.shape), full((1, 1))],
        out_specs=full(c0.shape),
        out_shape=jax.ShapeDtypeStruct(c0.shape, jnp.int32),
        name="fox_first_block",
    )(c0, ce, bound)


def _fox_kernel(first_ref, q_ref, k_ref, vt_ref, cum_ref, cumt_ref, o_ref, acc_ref, m_ref, csb_ref, *, t):
    h = pl.program_id(1)
    i = pl.program_id(2)
    seq = k_ref.shape[1]
    n_sub = t // LANES

    @pl.when(i == 0)
    def _():
        lane = lax.broadcasted_iota(jnp.int32, (t, LANES), 1)

        def fill(c, carry):
            st = pl.multiple_of(c * t, t)
            col = jnp.sum(jnp.where(lane == h, cum_ref[0, pl.ds(st, t), :], 0.0), axis=-1, keepdims=True)
            csb_ref[pl.ds(st, t), :] = jnp.broadcast_to(col, (t, LANES))
            return carry

        lax.fori_loop(0, seq // t, fill, 0)

    q = q_ref[0]
    ct = cumt_ref[0, 0, :, pl.ds(pl.multiple_of(i * t, t), t)]
    causal = (lax.broadcasted_iota(jnp.int32, (t, t), 0) <= lax.broadcasted_iota(jnp.int32, (t, t), 1))

    def scores(j, diagonal=False):
        st = pl.multiple_of(j * t, t)
        zt = _nt_dot(k_ref[0, pl.ds(st, t), :], q) + ct - _tile_lanes(csb_ref[pl.ds(st, t), :], n_sub)
        return jnp.where(causal, zt, NEG_BIG) if diagonal else zt

    def update(j, zt):
        vtb = vt_ref[0, 0, :, pl.ds(pl.multiple_of(j * t, t), t)]
        m_old = m_ref[...]
        m_new = jnp.maximum(m_old, jnp.max(zt, axis=0, keepdims=True))
        p = jnp.exp2(zt - m_new).astype(BF16)
        acc_ref[...] = jnp.exp2(m_old - m_new) * acc_ref[...] + _dot(vtb, p)
        m_ref[...] = m_new

    m_ref[...] = jnp.full_like(m_ref, NEG_BIG)
    acc_ref[...] = jnp.zeros_like(acc_ref)

    first = jnp.minimum(first_ref[pl.program_id(0) * N_HEADS + h, i], i)
    n_full = i - first

    def pair(jj, carry):
        ja = first + 2 * jj
        za = scores(ja)
        zb = scores(ja + 1)
        update(ja, za)
        update(ja + 1, zb)
        return carry

    lax.fori_loop(0, n_full // 2, pair, 0)

    @pl.when(n_full % 2 == 1)
    def _():
        update(i - 1, scores(i - 1))

    update(i, scores(i, True))
    o_ref[0] = jnp.transpose(acc_ref[0:HEAD_DIM, :] / acc_ref[HEAD_DIM:HEAD_DIM + 1, :]).astype(o_ref.dtype)


def _fox_attention(qkv, v_t, cum, cum_t, qk_bound, t=512):
    bsz, seq, _ = qkv.shape
    t = min(t, seq)
    dv = HEAD_DIM + ONES_ROWS
    first = _fox_first_block(cum_t, t, qk_bound)
    grid_spec = pltpu.PrefetchScalarGridSpec(
        num_scalar_prefetch=1,
        grid=(bsz, N_HEADS, seq // t),
        in_specs=[pl.BlockSpec((1, t, HEAD_DIM), lambda b, h, i, f: (b, i, h)),
                  pl.BlockSpec((1, seq, HEAD_DIM), lambda b, h, i, f: (b, 0, N_HEADS + h)),
                  pl.BlockSpec((1, 1, dv, seq), lambda b, h, i, f: (b, h, 0, 0)),
                  pl.BlockSpec((1, seq, LANES), lambda b, h, i, f: (b, 0, 0)),
                  pl.BlockSpec((1, 1, 1, seq), lambda b, h, i, f: (b, h, 0, 0))],
        out_specs=pl.BlockSpec((1, t, HEAD_DIM), lambda b, h, i, f: (b, i, h)),
        scratch_shapes=[pltpu.VMEM((dv, t), F32), pltpu.VMEM((1, t), F32), pltpu.VMEM((seq, LANES), F32)])
    return pl.pallas_call(
        functools.partial(_fox_kernel, t=t),
        grid_spec=grid_spec,
        out_shape=jax.ShapeDtypeStruct((bsz, seq, N_HEADS * HEAD_DIM), BF16),
        compiler_params=_cparams(("parallel", "parallel", "arbitrary")),
        name="fox_attn",
    )(first, qkv, qkv, v_t, cum, cum_t)


def _sortable(x):
    bits = pltpu.bitcast(x, jnp.int32)
    return jnp.where(bits < 0, bits ^ jnp.int32(0x7FFFFFFF), bits)


def _dsa_kernel(q_ref, k_ref, vt_ref, qi_ref, ki_ref, wi_ref, o_ref,
                key_ref, q4_ref, thr_ref, acc_ref, m_ref, *, t, k_top, count_rows):
    i = pl.program_id(1)
    n_blocks = i + 1
    group = N_HEADS // B_KV_HEADS
    qs = i * t

    for g in range(B_KV_HEADS):
        q4_ref[g] = jnp.concatenate(
            [q_ref[0, :, (g * group + r) * HEAD_DIM:(g * group + r + 1) * HEAD_DIM] for r in range(group)], axis=0)

    key_pos = lax.broadcasted_iota(jnp.int32, (t, t), 0)
    chunk_end = ((qs + lax.broadcasted_iota(jnp.int32, (1, t), 1)) // CHUNK + 1) * CHUNK

    def score_block(j, carry):
        st = pl.multiple_of(j * t, t)
        ka = ki_ref[0, pl.ds(st, t), 0:LANES]
        kb = ki_ref[0, pl.ds(st, t), LANES:2 * LANES]
        score = jnp.zeros((t, t), F32)
        for mblk in range(IDX_HEADS // 2):
            qm = qi_ref[0, :, mblk * LANES:(mblk + 1) * LANES]
            for half, kk in enumerate((ka, kb)):
                hd = 2 * mblk + half
                score = score + jnp.maximum(_nt_dot(kk, qm), 0.0) * wi_ref[0, hd:hd + 1, :]
        admissible = (st + key_pos) < chunk_end
        key_ref[pl.ds(st, t), :] = jnp.where(admissible, _sortable(score), KEY_NEG_INF)
        return carry

    lax.fori_loop(0, n_blocks, score_block, 0)

    def bit_body(it, lo):
        cand = lo + jnp.left_shift(jnp.int32(1), 31 - it)

        def count_body(j, cnt):
            st = pl.multiple_of(j * t, t)
            for u in range(t // count_rows):
                cnt = cnt + jnp.where(key_ref[pl.ds(st + u * count_rows, count_rows), :] >= cand, 1.0, 0.0)
            return cnt

        cnt = lax.fori_loop(0, n_blocks, count_body, jnp.zeros((count_rows, t), F32))
        total = jnp.sum(cnt, axis=0, keepdims=True)
        return jnp.where(total >= k_top, cand, lo)

    lo = lax.fori_loop(0, 32, bit_body, jnp.full((1, t), INT_MIN, jnp.int32))
    thr_ref[...] = jnp.maximum(lo, KEY_NEG_INF + 1)

    m_ref[...] = jnp.full_like(m_ref, NEG_BIG)
    acc_ref[...] = jnp.zeros_like(acc_ref)

    def attn_block(j, carry):
        st = pl.multiple_of(j * t, t)
        bias = jnp.where(key_ref[pl.ds(st, t), :] >= thr_ref[...], 0.0, NEG_BIG)
        bias4 = _tile_lanes(bias, group)
        scores = [_nt_dot(k_ref[0, pl.ds(st, t), g * HEAD_DIM:(g + 1) * HEAD_DIM], q4_ref[g]) + bias4
                  for g in range(B_KV_HEADS)]
        for g in range(B_KV_HEADS):
            vtb = vt_ref[0, g, :, pl.ds(st, t)]
            zt = scores[g]
            m_old = m_ref[g]
            m_new = jnp.maximum(m_old, jnp.max(zt, axis=0, keepdims=True))
            p = jnp.exp2(zt - m_new).astype(BF16)
            acc_ref[g] = jnp.exp2(m_old - m_new) * acc_ref[g] + _dot(vtb, p)
            m_ref[g] = m_new
        return carry

    lax.fori_loop(0, n_blocks, attn_block, 0)
    for g in range(B_KV_HEADS):
        out_t = acc_ref[g, 0:HEAD_DIM, :] / acc_ref[g, HEAD_DIM:HEAD_DIM + 1, :]
        for r in range(group):
            hd = g * group + r
            o_ref[0, :, hd * HEAD_DIM:(hd + 1) * HEAD_DIM] = jnp.transpose(
                out_t[:, r * t:(r + 1) * t]).astype(o_ref.dtype)


def _dsa_attention(qkv, v_t, qi, ki2, wi_t, k_top, t=256, count_rows=64):
    bsz, seq, _ = qkv.shape
    t = min(t, seq)
    count_rows = min(count_rows, t)
    dq = N_HEADS * HEAD_DIM
    dkv = B_KV_HEADS * HEAD_DIM
    dv = HEAD_DIM + ONES_ROWS
    group = N_HEADS // B_KV_HEADS
    resident = dict(pipeline_mode=pl.Buffered(1))
    return pl.pallas_call(
        functools.partial(_dsa_kernel, t=t, k_top=k_top, count_rows=count_rows),
        grid=(bsz, seq // t),
        in_specs=[pl.BlockSpec((1, t, dq), lambda b, i: (b, i, 0)),
                  pl.BlockSpec((1, seq, dkv), lambda b, i: (b, 0, dq // dkv), **resident),
                  pl.BlockSpec((1, B_KV_HEADS, dv, seq), lambda b, i: (b, 0, 0, 0), **resident),
                  pl.BlockSpec((1, t, IDX_HEADS * IDX_DIM), lambda b, i: (b, i, 0)),
                  pl.BlockSpec((1, seq, 2 * LANES), lambda b, i: (b, 0, 0), **resident),
                  pl.BlockSpec((1, IDX_HEADS, t), lambda b, i: (b, 0, i))],
        out_specs=pl.BlockSpec((1, t, dq), lambda b, i: (b, i, 0)),
        out_shape=jax.ShapeDtypeStruct((bsz, seq, dq), BF16),
        scratch_shapes=[pltpu.VMEM((seq, t), jnp.int32),
                        pltpu.VMEM((B_KV_HEADS, group * t, HEAD_DIM), BF16),
                        pltpu.VMEM((1, t), jnp.int32),
                        pltpu.VMEM((B_KV_HEADS, dv, group * t), F32),
                        pltpu.VMEM((B_KV_HEADS, 1, group * t), F32)],
        compiler_params=_cparams(("parallel", "arbitrary")),
        name="dsa_attn",
    )(qkv, qkv, v_t, qi, ki2, wi_t)


def _rope_tables(seq, rot_dim, period, active_lanes):
    half = rot_dim // 2
    inv = ROPE_THETA ** (-jnp.arange(half, dtype=F32) / half)
    ang = jnp.arange(seq, dtype=F32)[:, None] * inv[None, :]
    cos, sin = jnp.cos(ang), jnp.sin(ang)
    lane = jnp.arange(LANES)
    local = lane % period
    idx = local % half
    first = (local < half) & (lane < active_lanes)
    second = (local >= half) & (local < rot_dim) & (lane < active_lanes)
    c = jnp.where((first | second)[None, :], cos[:, idx], 1.0)
    s_up = jnp.where(first[None, :], -sin[:, idx], 0.0)
    s_dn = jnp.where(second[None, :], sin[:, idx], 0.0)
    return c, s_up, s_dn


def _rope_partner_matrix(rot_dim, period, n_lanes):
    half = rot_dim // 2
    lane = jnp.arange(n_lanes)
    local = lane % period
    src = lane[:, None]
    dst = lane[None, :]
    first = (local < half)[None, :] & (src == dst + half)
    second = ((local >= half) & (local < rot_dim))[None, :] & (src == dst - half)
    return (jnp.where(second, 1.0, 0.0) - jnp.where(first, 1.0, 0.0)).astype(BF16)


def _transpose_v(v, n_heads):
    bsz, seq, _ = v.shape
    v_t = jnp.transpose(v.reshape(bsz, seq, n_heads, HEAD_DIM), (0, 2, 3, 1))
    return jnp.concatenate([v_t, jnp.ones((bsz, n_heads, ONES_ROWS, seq), v.dtype)], axis=2)


def _pad_lanes(v, fill=0.0):
    return jnp.pad(v.astype(F32), (0, LANES - v.shape[0]), constant_values=fill).reshape(1, LANES)


def kernel(x, p, attn_norm, w_o, mlp_norm, w_up, w_down, ple_norm, w_ple_gate, w_ple_proj,
           a_w_in, a_q_norm, a_k_norm, b_w_in, b_q_norm, b_k_norm, b_idx_k_norm,
           c_w_in, c_b_f, c_q_norm, c_k_norm):
    bsz, seq, d = x.shape
    depth = attn_norm.shape[0]
    m = bsz * seq
    dm = N_HEADS * HEAD_DIM
    dkv = B_KV_HEADS * HEAD_DIM
    n_qi = IDX_HEADS * IDX_DIM
    scale = HEAD_DIM ** -0.5
    k_top = min(TOPK_MAX, seq // 4)

    h = x.reshape(m, d)
    for layer in range(depth):
        kind, j = layer % N_MIXERS, layer // N_MIXERS
        if kind == 0:
            gains = jnp.stack([a_q_norm[j] * scale, a_k_norm[j]])
            nt = dm // 1024
            qkv = _proj(h, attn_norm[layer], a_w_in[j].astype(BF16), gains, nt, nt, 1024)
            mix = _sb_attention(qkv.reshape(bsz, seq, 3 * dm))
        elif kind == 1:
            w = b_w_in[j]
            gains = jnp.stack([b_q_norm[j] * (scale * LOG2E), b_k_norm[j]])
            cos_tab, s_up, s_dn = _rope_tables(seq, ROPE_DIM, HEAD_DIM, LANES)
            main_tabs = (cos_tab, s_dn - s_up, _rope_partner_matrix(ROPE_DIM, HEAD_DIM, 2 * LANES))
            tn = dkv
            qkv = _proj(h, attn_norm[layer], w[:, :dm + 2 * dkv].astype(BF16), gains, dm // tn, 1, tn,
                        rope_tabs=main_tabs, rope_half=ROPE_DIM // 2, seq=seq)
            tail_w = IDX_DIM + IDX_HEADS
            w_idx = jnp.pad(w[:, dm + 2 * dkv:], ((0, 0), (0, LANES - tail_w))).astype(BF16)
            idx_tabs = _rope_tables(seq, IDX_ROPE_DIM, IDX_DIM, LANES)
            tail_tabs = _rope_tables(seq, IDX_ROPE_DIM, IDX_DIM, IDX_DIM)
            qi, tail = _proj_idx(h, attn_norm[layer], w_idx, _pad_lanes(b_idx_k_norm[j]),
                                 idx_tabs + tail_tabs, seq)
            ki = tail[:, :IDX_DIM].astype(BF16)
            zeros = jnp.zeros((m, LANES - IDX_DIM), BF16)
            ki2 = jnp.concatenate([ki, zeros, zeros, ki], axis=1).reshape(bsz, seq, 2 * LANES)
            wi_t = jnp.transpose(tail[:, IDX_DIM:tail_w].reshape(bsz, seq, IDX_HEADS), (0, 2, 1))
            qkv = qkv.reshape(bsz, seq, dm + 2 * dkv)
            v_t = _transpose_v(qkv[:, :, dm + dkv:], B_KV_HEADS)
            mix = _dsa_attention(qkv, v_t, qi.reshape(bsz, seq, n_qi), ki2, wi_t, k_top)
        else:
            w = c_w_in[j]
            gains = jnp.stack([c_q_norm[j] * (scale * LOG2E), c_k_norm[j]])
            nt = dm // 1024
            qkv = _proj(h, attn_norm[layer], w[:, :3 * dm].astype(BF16), gains, nt, nt, 1024)
            w_f = jnp.pad(w[:, 3 * dm:], ((0, 0), (0, LANES - N_HEADS))).astype(BF16)
            cum = _fgate_cumsum(h, attn_norm[layer], w_f, _pad_lanes(c_b_f[j]), seq)
            cum3 = cum.reshape(bsz, seq, LANES)
            cum_t = jnp.transpose(cum3[:, :, :N_HEADS], (0, 2, 1)).reshape(bsz, N_HEADS, 1, seq)
            qkv = qkv.reshape(bsz, seq, 3 * dm)
            v_t = _transpose_v(qkv[:, :, 2 * dm:], N_HEADS)
            qk_bound = (HEAD_DIM * scale * LOG2E) * jnp.max(jnp.abs(c_q_norm[j])) * jnp.max(jnp.abs(c_k_norm[j]))
            mix = _fox_attention(qkv, v_t, cum3, cum_t, qk_bound)
        h = _oproj(h, mix.reshape(m, dm), w_o[layer].astype(BF16))
        h = _mlp(h, mlp_norm[layer], w_up[layer].astype(BF16), w_down[layer].astype(BF16))
        h = _ple(h, ple_norm[layer], w_ple_gate[layer].astype(BF16), p[layer].reshape(m, -1),
                 w_ple_proj[layer].astype(BF16))
    return h.reshape(bsz, seq, d)
```

```python
import functools
import math

import jax
import jax.numpy as jnp
from jax import lax
from jax.experimental import pallas as pl
from jax.experimental.pallas import tpu as pltpu

F32 = jnp.float32
BF16 = jnp.bfloat16

N_HEADS = 16
HEAD_DIM = 128
ROPE_DIM = HEAD_DIM // 4
ROPE_THETA = 500000.0
B_KV_HEADS = 4
IDX_HEADS = 16
IDX_DIM = 64
IDX_ROPE_DIM = IDX_DIM // 4
TOPK_MAX = 256
CHUNK = 64
RMS_EPS = 1e-6
N_MIXERS = 3

LANES = 128
VMEM_LIMIT = 56 * 1024 * 1024
NEG_BIG = -1e30
EXP_UNDERFLOW = -104.0
LOG2E = math.log2(math.e)
ONES_ROWS = 16
INT_MIN = -(2 ** 31)
KEY_NEG_INF = INT_MIN + 0x7FFFFF


def _cparams(sem):
    return pltpu.CompilerParams(dimension_semantics=sem, vmem_limit_bytes=VMEM_LIMIT)


def _nt_dot(a, b):
    return lax.dot_general(a, b, (((1,), (1,)), ((), ())), preferred_element_type=F32)


def _dot(a, b):
    return jnp.dot(a, b, preferred_element_type=F32)


def _norm_rows(x_ref, g_ref, xn_ref, rows=256):
    tm = x_ref.shape[0]
    rows = min(rows, tm)
    for r in range(0, tm, rows):
        x = x_ref[r:r + rows, :]
        ms = jnp.mean(x * x, axis=-1, keepdims=True)
        xn_ref[r:r + rows, :] = (x * lax.rsqrt(ms + RMS_EPS) * g_ref[...]).astype(BF16)


def _head_norm(blk, gain_row):
    ms = jnp.mean(blk * blk, axis=-1, keepdims=True)
    return blk * lax.rsqrt(ms + RMS_EPS) * gain_row


def _rope(y, c, s_up, s_dn, half):
    return y * c + pltpu.roll(y, LANES - half, 1) * s_up + pltpu.roll(y, half, 1) * s_dn


def _proj_kernel(*refs, q_tiles, k_tiles, rope_half):
    if rope_half:
        x_ref, g_ref, w_ref, hg_ref, c_ref, s_ref, perm_ref, o_ref, xn_ref = refs
    else:
        x_ref, g_ref, w_ref, hg_ref, o_ref, xn_ref = refs
    j = pl.program_id(1)

    @pl.when(j == 0)
    def _():
        _norm_rows(x_ref, g_ref, xn_ref)

    acc = _dot(xn_ref[...], w_ref[...])
    tn = acc.shape[1]
    normed = j < q_tiles + k_tiles
    gain = hg_ref[pl.ds((j >= q_tiles).astype(jnp.int32), 1), :]
    heads = 2 if rope_half else 1
    width = heads * LANES
    for c in range(tn // width):
        blk = acc[:, c * width:(c + 1) * width]
        y = jnp.concatenate([_head_norm(blk[:, k * LANES:(k + 1) * LANES], gain) for k in range(heads)], axis=1)
        if rope_half:
            partner = _dot(y.astype(BF16), perm_ref[...])
            y = y * _tile_lanes(c_ref[...], heads) + partner * _tile_lanes(s_ref[...], heads)
        o_ref[:, c * width:(c + 1) * width] = jnp.where(normed, y, blk).astype(o_ref.dtype)


def _proj(h2, gain, w, head_gains, q_tiles, k_tiles, tn, rope_tabs=None, rope_half=0, seq=None, tm=1024):
    m, d = h2.shape
    n = w.shape[1]
    tm = min(tm, seq) if rope_half else min(tm, m)
    in_specs = [
        pl.BlockSpec((tm, d), lambda i, j: (i, 0)),
        pl.BlockSpec((1, d), lambda i, j: (0, 0)),
        pl.BlockSpec((d, tn), lambda i, j: (0, j)),
        pl.BlockSpec(head_gains.shape, lambda i, j: (0, 0)),
    ]
    args = [h2, gain.reshape(1, d), w, head_gains]
    if rope_half:
        tiles_per_seq = seq // tm
        cos_tab, sin_tab, perm = rope_tabs
        for t in (cos_tab, sin_tab):
            in_specs.append(pl.BlockSpec((tm, LANES), lambda i, j: (i % tiles_per_seq, 0)))
            args.append(t)
        in_specs.append(pl.BlockSpec(perm.shape, lambda i, j: (0, 0)))
        args.append(perm)
    return pl.pallas_call(
        functools.partial(_proj_kernel, q_tiles=q_tiles, k_tiles=k_tiles, rope_half=rope_half),
        grid=(m // tm, n // tn),
        in_specs=in_specs,
        out_specs=pl.BlockSpec((tm, tn), lambda i, j: (i, j)),
        out_shape=jax.ShapeDtypeStruct((m, n), BF16),
        scratch_shapes=[pltpu.VMEM((tm, d), BF16)],
        compiler_params=_cparams(("parallel", "arbitrary")),
        name="proj",
    )(*args)


def _proj_idx_kernel(x_ref, g_ref, w_ref, kg_ref, ci_ref, sui_ref, sdi_ref, ct_ref, sut_ref, sdt_ref,
                     qi_ref, tail_ref, xn_ref, *, wi_scale):
    _norm_rows(x_ref, g_ref, xn_ref)
    acc = _dot(xn_ref[...], w_ref[...])
    half = IDX_ROPE_DIM // 2
    n_qi = qi_ref.shape[1]
    for c in range(n_qi // LANES):
        y = _rope(acc[:, c * LANES:(c + 1) * LANES], ci_ref[...], sui_ref[...], sdi_ref[...], half)
        qi_ref[:, c * LANES:(c + 1) * LANES] = y.astype(qi_ref.dtype)
    t = acc[:, n_qi:n_qi + LANES]
    is_k = lax.broadcasted_iota(jnp.int32, t.shape, 1) < IDX_DIM
    ms = jnp.sum(jnp.where(is_k, t * t, 0.0), axis=-1, keepdims=True) * (1.0 / IDX_DIM)
    y = jnp.where(is_k, t * lax.rsqrt(ms + RMS_EPS) * kg_ref[...], t * wi_scale)
    tail_ref[...] = _rope(y, ct_ref[...], sut_ref[...], sdt_ref[...], half)


def _proj_idx(h2, gain, w, kgain_pad, tabs, seq, tm=512):
    m, d = h2.shape
    n = w.shape[1]
    n_qi = IDX_HEADS * IDX_DIM
    tm = min(tm, seq)
    tiles_per_seq = seq // tm
    const = lambda i: (0, 0)
    tab_spec = pl.BlockSpec((tm, LANES), lambda i: (i % tiles_per_seq, 0))
    return pl.pallas_call(
        functools.partial(_proj_idx_kernel, wi_scale=(IDX_HEADS ** -0.5) * (IDX_DIM ** -0.5)),
        grid=(m // tm,),
        in_specs=[pl.BlockSpec((tm, d), lambda i: (i, 0)), pl.BlockSpec((1, d), const),
                  pl.BlockSpec((d, n), const), pl.BlockSpec((1, LANES), const)] + [tab_spec] * 6,
        out_specs=[pl.BlockSpec((tm, n_qi), lambda i: (i, 0)), pl.BlockSpec((tm, LANES), lambda i: (i, 0))],
        out_shape=[jax.ShapeDtypeStruct((m, n_qi), BF16), jax.ShapeDtypeStruct((m, LANES), F32)],
        scratch_shapes=[pltpu.VMEM((tm, d), BF16)],
        compiler_params=_cparams(("parallel",)),
        name="proj_idx",
    )(h2, gain.reshape(1, d), w, kgain_pad, *tabs)


def _fgate_kernel(x_ref, g_ref, w_ref, b_ref, o_ref, xn_ref, carry_ref, *, tiles_per_seq):
    i = pl.program_id(0)
    _norm_rows(x_ref, g_ref, xn_ref)
    y = _dot(xn_ref[...], w_ref[...]) + b_ref[...]
    lf = (jnp.minimum(y, 0.0) - jnp.log1p(jnp.exp(-jnp.abs(y)))) * LOG2E
    tm = lf.shape[0]
    tri = (lax.broadcasted_iota(jnp.int32, (tm, tm), 0) >= lax.broadcasted_iota(jnp.int32, (tm, tm), 1))
    tri = jnp.where(tri, 1.0, 0.0).astype(BF16)
    a = lf.astype(BF16)
    r1 = lf - a.astype(F32)
    b = r1.astype(BF16)
    c = (r1 - b.astype(F32)).astype(BF16)
    cs = _dot(tri, a) + _dot(tri, b) + _dot(tri, c)

    @pl.when(i % tiles_per_seq == 0)
    def _():
        carry_ref[...] = jnp.zeros_like(carry_ref)

    out = cs + carry_ref[...]
    o_ref[...] = out
    carry_ref[...] = out[tm - 1:tm, :]


def _fgate_cumsum(h2, gain, w_pad, b_pad, seq, tm=512):
    m, d = h2.shape
    tm = min(tm, seq)
    const = lambda i: (0, 0)
    return pl.pallas_call(
        functools.partial(_fgate_kernel, tiles_per_seq=seq // tm),
        grid=(m // tm,),
        in_specs=[pl.BlockSpec((tm, d), lambda i: (i, 0)), pl.BlockSpec((1, d), const),
                  pl.BlockSpec((d, LANES), const), pl.BlockSpec((1, LANES), const)],
        out_specs=pl.BlockSpec((tm, LANES), lambda i: (i, 0)),
        out_shape=jax.ShapeDtypeStruct((m, LANES), F32),
        scratch_shapes=[pltpu.VMEM((tm, d), BF16), pltpu.VMEM((1, LANES), F32)],
        compiler_params=_cparams(("arbitrary",)),
        name="fgate_cumsum",
    )(h2, gain.reshape(1, d), w_pad, b_pad)


def _oproj_kernel(h_ref, mix_ref, w_ref, o_ref):
    o_ref[...] = h_ref[...] + _dot(mix_ref[...], w_ref[...])


def _oproj(h2, mix2, w, tm=512):
    m, d = h2.shape
    tm = min(tm, m)
    return pl.pallas_call(
        _oproj_kernel,
        grid=(m // tm,),
        in_specs=[pl.BlockSpec((tm, d), lambda i: (i, 0)), pl.BlockSpec((tm, d), lambda i: (i, 0)),
                  pl.BlockSpec((d, d), lambda i: (0, 0))],
        out_specs=pl.BlockSpec((tm, d), lambda i: (i, 0)),
        out_shape=jax.ShapeDtypeStruct((m, d), F32),
        compiler_params=_cparams(("parallel",)),
        name="oproj",
    )(h2, mix2, w)


def _mlp_kernel(x_ref, g_ref, wu_ref, wd_ref, o_ref, xn_ref):
    f = pl.program_id(1)

    @pl.when(f == 0)
    def _():
        _norm_rows(x_ref, g_ref, xn_ref)
        o_ref[...] = x_ref[...]

    a = jnp.maximum(_dot(xn_ref[...], wu_ref[...]), 0.0)
    o_ref[...] += _dot((a * a).astype(BF16), wd_ref[...])


def _mlp(h2, gain, w_up, w_down, tm=512, tf=1024):
    m, d = h2.shape
    dff = w_up.shape[1]
    tm = min(tm, m)
    return pl.pallas_call(
        _mlp_kernel,
        grid=(m // tm, dff // tf),
        in_specs=[pl.BlockSpec((tm, d), lambda i, f: (i, 0)), pl.BlockSpec((1, d), lambda i, f: (0, 0)),
                  pl.BlockSpec((d, tf), lambda i, f: (0, f)), pl.BlockSpec((tf, d), lambda i, f: (f, 0))],
        out_specs=pl.BlockSpec((tm, d), lambda i, f: (i, 0)),
        out_shape=jax.ShapeDtypeStruct((m, d), F32),
        scratch_shapes=[pltpu.VMEM((tm, d), BF16)],
        compiler_params=_cparams(("parallel", "arbitrary")),
        name="mlp",
    )(h2, gain.reshape(1, d), w_up, w_down)


def _ple_kernel(x_ref, g_ref, wg_ref, p_ref, wp_ref, o_ref, xn_ref):
    _norm_rows(x_ref, g_ref, xn_ref)
    gate = jax.nn.sigmoid(_dot(xn_ref[...], wg_ref[...]))
    emb = _dot(p_ref[...].astype(BF16), wp_ref[...])
    o_ref[...] = x_ref[...] + gate * emb


def _ple(h2, gain, w_gate, p2, w_proj, tm=512):
    m, d = h2.shape
    pd = p2.shape[1]
    tm = min(tm, m)
    const = lambda i: (0, 0)
    return pl.pallas_call(
        _ple_kernel,
        grid=(m // tm,),
        in_specs=[pl.BlockSpec((tm, d), lambda i: (i, 0)), pl.BlockSpec((1, d), const),
                  pl.BlockSpec((d, d), const), pl.BlockSpec((tm, pd), lambda i: (i, 0)),
                  pl.BlockSpec((pd, d), const)],
        out_specs=pl.BlockSpec((tm, d), lambda i: (i, 0)),
        out_shape=jax.ShapeDtypeStruct((m, d), F32),
        scratch_shapes=[pltpu.VMEM((tm, d), BF16)],
        compiler_params=_cparams(("parallel",)),
        name="ple",
    )(h2, gain.reshape(1, d), w_gate, p2, w_proj)


def _sb_kernel(q_ref, k_ref, v_ref, o_ref, acc_ref, aft_ref, *, tq, heads):
    i = pl.program_id(2)
    row = lax.broadcasted_iota(jnp.int32, (tq, tq), 0)
    col = lax.broadcasted_iota(jnp.int32, (tq, tq), 1)
    later = jnp.where(row > col, 1.0, 0.0).astype(BF16)
    strict = col < row

    def block(hd, j, diagonal):
        start = pl.multiple_of(j * tq, tq)
        cols = slice(hd * HEAD_DIM, (hd + 1) * HEAD_DIM)
        kb = k_ref[0, pl.ds(start, tq), cols]
        vb = v_ref[0, pl.ds(start, tq), cols]
        z = _nt_dot(q_ref[0, :, cols], kb)
        sp = jnp.maximum(z, 0.0) + jnp.log(1.0 + jnp.exp(-jnp.abs(z)))
        log_1mb = -sp
        if diagonal:
            log_1mb = jnp.where(strict, log_1mb, 0.0)
        hi = log_1mb.astype(BF16)
        lo = (log_1mb - hi.astype(F32)).astype(BF16)
        log_w = (z - sp) + _dot(hi, later) + _dot(lo, later)
        return log_w, jnp.sum(log_1mb, axis=-1, keepdims=True), vb

    prev = jnp.maximum(i - 1, 0)
    has_prev = (i > 0).astype(F32)
    worst = None
    for hd in range(heads):
        log_w0, sum0, v0 = block(hd, i, True)
        log_w1, sum1, v1 = block(hd, prev, False)
        w0 = jnp.where(strict, jnp.exp(log_w0), 0.0)
        w1 = jnp.exp(log_w1 + sum0) * has_prev
        acc_ref[hd] = _dot(w0.astype(BF16), v0) + _dot(w1.astype(BF16), v1)
        aft = sum0 + sum1 * has_prev
        aft_ref[hd] = aft
        worst = jnp.max(aft) if worst is None else jnp.maximum(worst, jnp.max(aft))

    def cond(c):
        j, worst = c
        return jnp.logical_and(j >= 0, worst > EXP_UNDERFLOW)

    def body(c):
        j, _ = c
        worst = None
        for hd in range(heads):
            log_w, row_sum, vb = block(hd, j, False)
            acc_ref[hd] += _dot(jnp.exp(log_w + aft_ref[hd]).astype(BF16), vb)
            aft = aft_ref[hd] + row_sum
            aft_ref[hd] = aft
            worst = jnp.max(aft) if worst is None else jnp.maximum(worst, jnp.max(aft))
        return j - 1, worst

    lax.while_loop(cond, body, (i - 2, worst))
    for hd in range(heads):
        o_ref[0, :, hd * HEAD_DIM:(hd + 1) * HEAD_DIM] = acc_ref[hd].astype(o_ref.dtype)


def _sb_attention(qkv, tq=256, heads=4):
    bsz, seq, _ = qkv.shape
    tq = min(tq, seq)
    width = heads * HEAD_DIM
    n_groups = N_HEADS // heads
    return pl.pallas_call(
        functools.partial(_sb_kernel, tq=tq, heads=heads),
        grid=(bsz, n_groups, seq // tq),
        in_specs=[pl.BlockSpec((1, tq, width), lambda b, h, i: (b, i, h)),
                  pl.BlockSpec((1, seq, width), lambda b, h, i: (b, 0, n_groups + h)),
                  pl.BlockSpec((1, seq, width), lambda b, h, i: (b, 0, 2 * n_groups + h))],
        out_specs=pl.BlockSpec((1, tq, width), lambda b, h, i: (b, i, h)),
        out_shape=jax.ShapeDtypeStruct((bsz, seq, N_HEADS * HEAD_DIM), BF16),
        scratch_shapes=[pltpu.VMEM((heads, tq, HEAD_DIM), F32), pltpu.VMEM((heads, tq, 1), F32)],
        compiler_params=_cparams(("parallel", "parallel", "arbitrary")),
        name="sb_attn",
    )(qkv, qkv, qkv)


def _tile_lanes(x, n):
    return x if n == 1 else jnp.concatenate([x] * n, axis=1)


def _fox_first_block_kernel(c0_ref, ce_ref, bound_ref, o_ref):
    c0 = c0_ref[...]
    n_blocks = ce_ref.shape[1]
    first = jnp.full(c0.shape, n_blocks, jnp.int32)
    for j in range(n_blocks):
        needed = c0 - ce_ref[:, j:j + 1] >= -bound_ref[...]
        first = jnp.minimum(first, jnp.where(needed, j, n_blocks))
    o_ref[...] = first


def _fox_first_block(cum_t, tq, tk, qk_bound):
    bsz, n_heads, _, seq = cum_t.shape
    rows = bsz * n_heads
    c0 = cum_t[:, :, 0, ::tq].reshape(rows, seq // tq)
    ce = cum_t[:, :, 0, tk - 1::tk].reshape(rows, seq // tk)
    bound = (151.0 + 2.04 * qk_bound).astype(F32).reshape(1, 1)
    full = lambda shape: pl.BlockSpec(shape, lambda: (0,) * len(shape))
    return pl.pallas_call(
        _fox_first_block_kernel,
        in_specs=[full(c0.shape), full(ce.shape), full((1, 1))],
        out_specs=full(c0.shape),
        out_shape=jax.ShapeDtypeStruct(c0.shape, jnp.int32),
        name="fox_first_block",
    )(c0, ce, bound)


def _fox_kernel(first_ref, q_ref, k_ref, vt_ref, cum_ref, cumt_ref, o_ref, acc_ref, m_ref, csb_ref, *, tq, tk):
    h = pl.program_id(1)
    i = pl.program_id(2)
    seq = k_ref.shape[1]
    ratio = tq // tk

    @pl.when(i == 0)
    def _():
        lane = lax.broadcasted_iota(jnp.int32, (tk, LANES), 1)

        def fill(c, carry):
            st = pl.multiple_of(c * tk, tk)
            col = jnp.sum(jnp.where(lane == h, cum_ref[0, pl.ds(st, tk), :], 0.0), axis=-1, keepdims=True)
            csb_ref[pl.ds(st, tk), :] = jnp.broadcast_to(col, (tk, LANES))
            return carry

        lax.fori_loop(0, seq // tk, fill, 0)

    q = q_ref[0]
    ct = cumt_ref[0, 0, :, pl.ds(pl.multiple_of(i * tq, tq), tq)]
    key_pos = lax.broadcasted_iota(jnp.int32, (tk, tq), 0)
    query_pos = lax.broadcasted_iota(jnp.int32, (tk, tq), 1)

    def scores(j, diagonal=None):
        st = pl.multiple_of(j * tk, tk)
        zt = _nt_dot(k_ref[0, pl.ds(st, tk), :], q) + ct - _tile_lanes(csb_ref[pl.ds(st, tk), :], tq // LANES)
        if diagonal is None:
            return zt
        return jnp.where(key_pos + diagonal * tk <= query_pos, zt, NEG_BIG)

    def update(j, zt):
        vtb = vt_ref[0, 0, :, pl.ds(pl.multiple_of(j * tk, tk), tk)]
        m_old = m_ref[...]
        m_new = jnp.maximum(m_old, jnp.max(zt, axis=0, keepdims=True))
        p = jnp.exp2(zt - m_new).astype(BF16)
        acc_ref[...] = jnp.exp2(m_old - m_new) * acc_ref[...] + _dot(vtb, p)
        m_ref[...] = m_new

    m_ref[...] = jnp.full_like(m_ref, NEG_BIG)
    acc_ref[...] = jnp.zeros_like(acc_ref)

    below = i * ratio
    first = jnp.minimum(first_ref[pl.program_id(0) * N_HEADS + h, i], below)
    n_full = below - first

    def pair(jj, carry):
        ja = first + 2 * jj
        za = scores(ja)
        zb = scores(ja + 1)
        update(ja, za)
        update(ja + 1, zb)
        return carry

    lax.fori_loop(0, n_full // 2, pair, 0)

    @pl.when(n_full % 2 == 1)
    def _():
        update(below - 1, scores(below - 1))

    for d in range(ratio):
        update(below + d, scores(below + d, d))
    o_ref[0] = jnp.transpose(acc_ref[0:HEAD_DIM, :] / acc_ref[HEAD_DIM:HEAD_DIM + 1, :]).astype(o_ref.dtype)


def _fox_attention(qkv, v_t, cum, cum_t, qk_bound, tq=1024, tk=512):
    bsz, seq, _ = qkv.shape
    tq = min(tq, seq)
    tk = min(tk, tq)
    dv = HEAD_DIM + ONES_ROWS
    first = _fox_first_block(cum_t, tq, tk, qk_bound)
    grid_spec = pltpu.PrefetchScalarGridSpec(
        num_scalar_prefetch=1,
        grid=(bsz, N_HEADS, seq // tq),
        in_specs=[pl.BlockSpec((1, tq, HEAD_DIM), lambda b, h, i, f: (b, i, h)),
                  pl.BlockSpec((1, seq, HEAD_DIM), lambda b, h, i, f: (b, 0, N_HEADS + h)),
                  pl.BlockSpec((1, 1, dv, seq), lambda b, h, i, f: (b, h, 0, 0)),
                  pl.BlockSpec((1, seq, LANES), lambda b, h, i, f: (b, 0, 0)),
                  pl.BlockSpec((1, 1, 1, seq), lambda b, h, i, f: (b, h, 0, 0))],
        out_specs=pl.BlockSpec((1, tq, HEAD_DIM), lambda b, h, i, f: (b, i, h)),
        scratch_shapes=[pltpu.VMEM((dv, tq), F32), pltpu.VMEM((1, tq), F32), pltpu.VMEM((seq, LANES), F32)])
    return pl.pallas_call(
        functools.partial(_fox_kernel, tq=tq, tk=tk),
        grid_spec=grid_spec,
        out_shape=jax.ShapeDtypeStruct((bsz, seq, N_HEADS * HEAD_DIM), BF16),
        compiler_params=_cparams(("parallel", "parallel", "arbitrary")),
        name="fox_attn",
    )(first, qkv, qkv, v_t, cum, cum_t)


def _sortable(x):
    bits = pltpu.bitcast(x, jnp.int32)
    return jnp.where(bits < 0, bits ^ jnp.int32(0x7FFFFFFF), bits)


def _dsa_kernel(q_ref, k_ref, vt_ref, qi_ref, ki_ref, wi_ref, o_ref,
                key_ref, q4_ref, thr_ref, acc_ref, m_ref, *, t, k_top, count_rows):
    i = pl.program_id(1)
    n_blocks = i + 1
    group = N_HEADS // B_KV_HEADS
    qs = i * t

    for g in range(B_KV_HEADS):
        q4_ref[g] = jnp.concatenate(
            [q_ref[0, :, (g * group + r) * HEAD_DIM:(g * group + r + 1) * HEAD_DIM] for r in range(group)], axis=0)

    key_pos = lax.broadcasted_iota(jnp.int32, (t, t), 0)
    chunk_end = ((qs + lax.broadcasted_iota(jnp.int32, (1, t), 1)) // CHUNK + 1) * CHUNK

    def score_block(j, carry):
        st = pl.multiple_of(j * t, t)
        ka = ki_ref[0, pl.ds(st, t), 0:LANES]
        kb = ki_ref[0, pl.ds(st, t), LANES:2 * LANES]
        score = jnp.zeros((t, t), F32)
        for mblk in range(IDX_HEADS // 2):
            qm = qi_ref[0, :, mblk * LANES:(mblk + 1) * LANES]
            for half, kk in enumerate((ka, kb)):
                hd = 2 * mblk + half
                score = score + jnp.maximum(_nt_dot(kk, qm), 0.0) * wi_ref[0, hd:hd + 1, :]
        admissible = (st + key_pos) < chunk_end
        key_ref[pl.ds(st, t), :] = jnp.where(admissible, _sortable(score), KEY_NEG_INF)
        return carry

    lax.fori_loop(0, n_blocks, score_block, 0)

    def bit_body(it, lo):
        cand = lo + jnp.left_shift(jnp.int32(1), 31 - it)

        def count_body(j, cnt):
            st = pl.multiple_of(j * t, t)
            for u in range(t // count_rows):
                cnt = cnt + jnp.where(key_ref[pl.ds(st + u * count_rows, count_rows), :] >= cand, 1.0, 0.0)
            return cnt

        cnt = lax.fori_loop(0, n_blocks, count_body, jnp.zeros((count_rows, t), F32))
        total = jnp.sum(cnt, axis=0, keepdims=True)
        return jnp.where(total >= k_top, cand, lo)

    lo = lax.fori_loop(0, 32, bit_body, jnp.full((1, t), INT_MIN, jnp.int32))
    thr_ref[...] = jnp.maximum(lo, KEY_NEG_INF + 1)

    m_ref[...] = jnp.full_like(m_ref, NEG_BIG)
    acc_ref[...] = jnp.zeros_like(acc_ref)

    def attn_block(j, carry):
        st = pl.multiple_of(j * t, t)
        bias = jnp.where(key_ref[pl.ds(st, t), :] >= thr_ref[...], 0.0, NEG_BIG)
        bias4 = _tile_lanes(bias, group)
        scores = [_nt_dot(k_ref[0, pl.ds(st, t), g * HEAD_DIM:(g + 1) * HEAD_DIM], q4_ref[g]) + bias4
                  for g in range(B_KV_HEADS)]
        for g in range(B_KV_HEADS):
            vtb = vt_ref[0, g, :, pl.ds(st, t)]
            zt = scores[g]
            m_old = m_ref[g]
            m_new = jnp.maximum(m_old, jnp.max(zt, axis=0, keepdims=True))
            p = jnp.exp2(zt - m_new).astype(BF16)
            acc_ref[g] = jnp.exp2(m_old - m_new) * acc_ref[g] + _dot(vtb, p)
            m_ref[g] = m_new
        return carry

    lax.fori_loop(0, n_blocks, attn_block, 0)
    for g in range(B_KV_HEADS):
        out_t = acc_ref[g, 0:HEAD_DIM, :] / acc_ref[g, HEAD_DIM:HEAD_DIM + 1, :]
        for r in range(group):
            hd = g * group + r
            o_ref[0, :, hd * HEAD_DIM:(hd + 1) * HEAD_DIM] = jnp.transpose(
                out_t[:, r * t:(r + 1) * t]).astype(o_ref.dtype)


def _dsa_attention(qkv, v_t, qi, ki2, wi_t, k_top, t=256, count_rows=64):
    bsz, seq, _ = qkv.shape
    t = min(t, seq)
    count_rows = min(count_rows, t)
    dq = N_HEADS * HEAD_DIM
    dkv = B_KV_HEADS * HEAD_DIM
    dv = HEAD_DIM + ONES_ROWS
    group = N_HEADS // B_KV_HEADS
    resident = dict(pipeline_mode=pl.Buffered(1))
    return pl.pallas_call(
        functools.partial(_dsa_kernel, t=t, k_top=k_top, count_rows=count_rows),
        grid=(bsz, seq // t),
        in_specs=[pl.BlockSpec((1, t, dq), lambda b, i: (b, i, 0)),
                  pl.BlockSpec((1, seq, dkv), lambda b, i: (b, 0, dq // dkv), **resident),
                  pl.BlockSpec((1, B_KV_HEADS, dv, seq), lambda b, i: (b, 0, 0, 0), **resident),
                  pl.BlockSpec((1, t, IDX_HEADS * IDX_DIM), lambda b, i: (b, i, 0)),
                  pl.BlockSpec((1, seq, 2 * LANES), lambda b, i: (b, 0, 0), **resident),
                  pl.BlockSpec((1, IDX_HEADS, t), lambda b, i: (b, 0, i))],
        out_specs=pl.BlockSpec((1, t, dq), lambda b, i: (b, i, 0)),
        out_shape=jax.ShapeDtypeStruct((bsz, seq, dq), BF16),
        scratch_shapes=[pltpu.VMEM((seq, t), jnp.int32),
                        pltpu.VMEM((B_KV_HEADS, group * t, HEAD_DIM), BF16),
                        pltpu.VMEM((1, t), jnp.int32),
                        pltpu.VMEM((B_KV_HEADS, dv, group * t), F32),
                        pltpu.VMEM((B_KV_HEADS, 1, group * t), F32)],
        compiler_params=_cparams(("parallel", "arbitrary")),
        name="dsa_attn",
    )(qkv, qkv, v_t, qi, ki2, wi_t)


def _rope_tables(seq, rot_dim, period, active_lanes):
    half = rot_dim // 2
    inv = ROPE_THETA ** (-jnp.arange(half, dtype=F32) / half)
    ang = jnp.arange(seq, dtype=F32)[:, None] * inv[None, :]
    cos, sin = jnp.cos(ang), jnp.sin(ang)
    lane = jnp.arange(LANES)
    local = lane % period
    idx = local % half
    first = (local < half) & (lane < active_lanes)
    second = (local >= half) & (local < rot_dim) & (lane < active_lanes)
    c = jnp.where((first | second)[None, :], cos[:, idx], 1.0)
    s_up = jnp.where(first[None, :], -sin[:, idx], 0.0)
    s_dn = jnp.where(second[None, :], sin[:, idx], 0.0)
    return c, s_up, s_dn


def _rope_partner_matrix(rot_dim, period, n_lanes):
    half = rot_dim // 2
    lane = jnp.arange(n_lanes)
    local = lane % period
    src = lane[:, None]
    dst = lane[None, :]
    first = (local < half)[None, :] & (src == dst + half)
    second = ((local >= half) & (local < rot_dim))[None, :] & (src == dst - half)
    return (jnp.where(second, 1.0, 0.0) - jnp.where(first, 1.0, 0.0)).astype(BF16)


def _transpose_v(v, n_heads):
    bsz, seq, _ = v.shape
    v_t = jnp.transpose(v.reshape(bsz, seq, n_heads, HEAD_DIM), (0, 2, 3, 1))
    return jnp.concatenate([v_t, jnp.ones((bsz, n_heads, ONES_ROWS, seq), v.dtype)], axis=2)


def _pad_lanes(v, fill=0.0):
    return jnp.pad(v.astype(F32), (0, LANES - v.shape[0]), constant_values=fill).reshape(1, LANES)


def kernel(x, p, attn_norm, w_o, mlp_norm, w_up, w_down, ple_norm, w_ple_gate, w_ple_proj,
           a_w_in, a_q_norm, a_k_norm, b_w_in, b_q_norm, b_k_norm, b_idx_k_norm,
           c_w_in, c_b_f, c_q_norm, c_k_norm):
    bsz, seq, d = x.shape
    depth = attn_norm.shape[0]
    m = bsz * seq
    dm = N_HEADS * HEAD_DIM
    dkv = B_KV_HEADS * HEAD_DIM
    n_qi = IDX_HEADS * IDX_DIM
    scale = HEAD_DIM ** -0.5
    k_top = min(TOPK_MAX, seq // 4)

    h = x.reshape(m, d)
    for layer in range(depth):
        kind, j = layer % N_MIXERS, layer // N_MIXERS
        if kind == 0:
            gains = jnp.stack([a_q_norm[j] * scale, a_k_norm[j]])
            nt = dm // 1024
            qkv = _proj(h, attn_norm[layer], a_w_in[j].astype(BF16), gains, nt, nt, 1024)
            mix = _sb_attention(qkv.reshape(bsz, seq, 3 * dm))
        elif kind == 1:
            w = b_w_in[j]
            gains = jnp.stack([b_q_norm[j] * (scale * LOG2E), b_k_norm[j]])
            cos_tab, s_up, s_dn = _rope_tables(seq, ROPE_DIM, HEAD_DIM, LANES)
            main_tabs = (cos_tab, s_dn - s_up, _rope_partner_matrix(ROPE_DIM, HEAD_DIM, 2 * LANES))
            tn = dkv
            qkv = _proj(h, attn_norm[layer], w[:, :dm + 2 * dkv].astype(BF16), gains, dm // tn, 1, tn,
                        rope_tabs=main_tabs, rope_half=ROPE_DIM // 2, seq=seq)
            tail_w = IDX_DIM + IDX_HEADS
            w_idx = jnp.pad(w[:, dm + 2 * dkv:], ((0, 0), (0, LANES - tail_w))).astype(BF16)
            idx_tabs = _rope_tables(seq, IDX_ROPE_DIM, IDX_DIM, LANES)
            tail_tabs = _rope_tables(seq, IDX_ROPE_DIM, IDX_DIM, IDX_DIM)
            qi, tail = _proj_idx(h, attn_norm[layer], w_idx, _pad_lanes(b_idx_k_norm[j]),
                                 idx_tabs + tail_tabs, seq)
            ki = tail[:, :IDX_DIM].astype(BF16)
            zeros = jnp.zeros((m, LANES - IDX_DIM), BF16)
            ki2 = jnp.concatenate([ki, zeros, zeros, ki], axis=1).reshape(bsz, seq, 2 * LANES)
            wi_t = jnp.transpose(tail[:, IDX_DIM:tail_w].reshape(bsz, seq, IDX_HEADS), (0, 2, 1))
            qkv = qkv.reshape(bsz, seq, dm + 2 * dkv)
            v_t = _transpose_v(qkv[:, :, dm + dkv:], B_KV_HEADS)
            mix = _dsa_attention(qkv, v_t, qi.reshape(bsz, seq, n_qi), ki2, wi_t, k_top)
        else:
            w = c_w_in[j]
            gains = jnp.stack([c_q_norm[j] * (scale * LOG2E), c_k_norm[j]])
            nt = dm // 1024
            qkv = _proj(h, attn_norm[layer], w[:, :3 * dm].astype(BF16), gains, nt, nt, 1024)
            w_f = jnp.pad(w[:, 3 * dm:], ((0, 0), (0, LANES - N_HEADS))).astype(BF16)
            cum = _fgate_cumsum(h, attn_norm[layer], w_f, _pad_lanes(c_b_f[j]), seq)
            cum3 = cum.reshape(bsz, seq, LANES)
            cum_t = jnp.transpose(cum3[:, :, :N_HEADS], (0, 2, 1)).reshape(bsz, N_HEADS, 1, seq)
            qkv = qkv.reshape(bsz, seq, 3 * dm)
            v_t = _transpose_v(qkv[:, :, 2 * dm:], N_HEADS)
            qk_bound = (HEAD_DIM * scale * LOG2E) * jnp.max(jnp.abs(c_q_norm[j])) * jnp.max(jnp.abs(c_k_norm[j]))
            mix = _fox_attention(qkv, v_t, cum3, cum_t, qk_bound)
        h = _oproj(h, mix.reshape(m, dm), w_o[layer].astype(BF16))
        h = _mlp(h, mlp_norm[layer], w_up[layer].astype(BF16), w_down[layer].astype(BF16))
        h = _ple(h, ple_norm[layer], w_ple_gate[layer].astype(BF16), p[layer].reshape(m, -1),
                 w_ple_proj[layer].astype(BF16))
    return h.reshape(bsz, seq, d)
```

```python
import functools
import math

import jax
import jax.numpy as jnp
from jax import lax
from jax.experimental import pallas as pl
from jax.experimental.pallas import tpu as pltpu

F32 = jnp.float32
BF16 = jnp.bfloat16

N_HEADS = 16
HEAD_DIM = 128
ROPE_DIM = HEAD_DIM // 4
ROPE_THETA = 500000.0
B_KV_HEADS = 4
IDX_HEADS = 16
IDX_DIM = 64
IDX_ROPE_DIM = IDX_DIM // 4
TOPK_MAX = 256
CHUNK = 64
RMS_EPS = 1e-6
N_MIXERS = 3

LANES = 128
VMEM_LIMIT = 56 * 1024 * 1024
NEG_BIG = -1e30
EXP_UNDERFLOW = -104.0
LOG2E = math.log2(math.e)
ONES_ROWS = 16
INT_MIN = -(2 ** 31)
KEY_NEG_INF = INT_MIN + 0x7FFFFF


def _cparams(sem):
    return pltpu.CompilerParams(dimension_semantics=sem, vmem_limit_bytes=VMEM_LIMIT)


def _nt_dot(a, b):
    return lax.dot_general(a, b, (((1,), (1,)), ((), ())), preferred_element_type=F32)


def _dot(a, b):
    return jnp.dot(a, b, preferred_element_type=F32)


def _norm_rows(x_ref, g_ref, xn_ref, rows=256):
    tm = x_ref.shape[0]
    rows = min(rows, tm)
    for r in range(0, tm, rows):
        x = x_ref[r:r + rows, :]
        ms = jnp.mean(x * x, axis=-1, keepdims=True)
        xn_ref[r:r + rows, :] = (x * lax.rsqrt(ms + RMS_EPS) * g_ref[...]).astype(BF16)


def _head_norm(blk, gain_row):
    ms = jnp.mean(blk * blk, axis=-1, keepdims=True)
    return blk * lax.rsqrt(ms + RMS_EPS) * gain_row


def _rope(y, c, s_up, s_dn, half):
    return y * c + pltpu.roll(y, LANES - half, 1) * s_up + pltpu.roll(y, half, 1) * s_dn


def _proj_kernel(*refs, q_tiles, k_tiles, rope_half):
    if rope_half:
        x_ref, g_ref, w_ref, hg_ref, c_ref, s_ref, perm_ref, o_ref, xn_ref = refs
    else:
        x_ref, g_ref, w_ref, hg_ref, o_ref, xn_ref = refs
    j = pl.program_id(1)

    @pl.when(j == 0)
    def _():
        _norm_rows(x_ref, g_ref, xn_ref)

    acc = _dot(xn_ref[...], w_ref[...])
    tn = acc.shape[1]
    normed = j < q_tiles + k_tiles
    gain = hg_ref[pl.ds((j >= q_tiles).astype(jnp.int32), 1), :]
    heads = 2 if rope_half else 1
    width = heads * LANES
    for c in range(tn // width):
        blk = acc[:, c * width:(c + 1) * width]
        y = jnp.concatenate([_head_norm(blk[:, k * LANES:(k + 1) * LANES], gain) for k in range(heads)], axis=1)
        if rope_half:
            partner = _dot(y.astype(BF16), perm_ref[...])
            y = y * _tile_lanes(c_ref[...], heads) + partner * _tile_lanes(s_ref[...], heads)
        o_ref[:, c * width:(c + 1) * width] = jnp.where(normed, y, blk).astype(o_ref.dtype)


def _proj(h2, gain, w, head_gains, q_tiles, k_tiles, tn, rope_tabs=None, rope_half=0, seq=None, tm=1024):
    m, d = h2.shape
    n = w.shape[1]
    tm = min(tm, seq) if rope_half else min(tm, m)
    in_specs = [
        pl.BlockSpec((tm, d), lambda i, j: (i, 0)),
        pl.BlockSpec((1, d), lambda i, j: (0, 0)),
        pl.BlockSpec((d, tn), lambda i, j: (0, j)),
        pl.BlockSpec(head_gains.shape, lambda i, j: (0, 0)),
    ]
    args = [h2, gain.reshape(1, d), w, head_gains]
    if rope_half:
        tiles_per_seq = seq // tm
        cos_tab, sin_tab, perm = rope_tabs
        for t in (cos_tab, sin_tab):
            in_specs.append(pl.BlockSpec((tm, LANES), lambda i, j: (i % tiles_per_seq, 0)))
            args.append(t)
        in_specs.append(pl.BlockSpec(perm.shape, lambda i, j: (0, 0)))
        args.append(perm)
    return pl.pallas_call(
        functools.partial(_proj_kernel, q_tiles=q_tiles, k_tiles=k_tiles, rope_half=rope_half),
        grid=(m // tm, n // tn),
        in_specs=in_specs,
        out_specs=pl.BlockSpec((tm, tn), lambda i, j: (i, j)),
        out_shape=jax.ShapeDtypeStruct((m, n), BF16),
        scratch_shapes=[pltpu.VMEM((tm, d), BF16)],
        compiler_params=_cparams(("parallel", "arbitrary")),
        name="proj",
    )(*args)


def _proj_idx_kernel(x_ref, g_ref, w_ref, kg_ref, ci_ref, sui_ref, sdi_ref, ct_ref, sut_ref, sdt_ref,
                     qi_ref, tail_ref, xn_ref, *, wi_scale):
    _norm_rows(x_ref, g_ref, xn_ref)
    acc = _dot(xn_ref[...], w_ref[...])
    half = IDX_ROPE_DIM // 2
    n_qi = qi_ref.shape[1]
    for c in range(n_qi // LANES):
        y = _rope(acc[:, c * LANES:(c + 1) * LANES], ci_ref[...], sui_ref[...], sdi_ref[...], half)
        qi_ref[:, c * LANES:(c + 1) * LANES] = y.astype(qi_ref.dtype)
    t = acc[:, n_qi:n_qi + LANES]
    is_k = lax.broadcasted_iota(jnp.int32, t.shape, 1) < IDX_DIM
    ms = jnp.sum(jnp.where(is_k, t * t, 0.0), axis=-1, keepdims=True) * (1.0 / IDX_DIM)
    y = jnp.where(is_k, t * lax.rsqrt(ms + RMS_EPS) * kg_ref[...], t * wi_scale)
    tail_ref[...] = _rope(y, ct_ref[...], sut_ref[...], sdt_ref[...], half)


def _proj_idx(h2, gain, w, kgain_pad, tabs, seq, tm=512):
    m, d = h2.shape
    n = w.shape[1]
    n_qi = IDX_HEADS * IDX_DIM
    tm = min(tm, seq)
    tiles_per_seq = seq // tm
    const = lambda i: (0, 0)
    tab_spec = pl.BlockSpec((tm, LANES), lambda i: (i % tiles_per_seq, 0))
    return pl.pallas_call(
        functools.partial(_proj_idx_kernel, wi_scale=(IDX_HEADS ** -0.5) * (IDX_DIM ** -0.5)),
        grid=(m // tm,),
        in_specs=[pl.BlockSpec((tm, d), lambda i: (i, 0)), pl.BlockSpec((1, d), const),
                  pl.BlockSpec((d, n), const), pl.BlockSpec((1, LANES), const)] + [tab_spec] * 6,
        out_specs=[pl.BlockSpec((tm, n_qi), lambda i: (i, 0)), pl.BlockSpec((tm, LANES), lambda i: (i, 0))],
        out_shape=[jax.ShapeDtypeStruct((m, n_qi), BF16), jax.ShapeDtypeStruct((m, LANES), F32)],
        scratch_shapes=[pltpu.VMEM((tm, d), BF16)],
        compiler_params=_cparams(("parallel",)),
        name="proj_idx",
    )(h2, gain.reshape(1, d), w, kgain_pad, *tabs)


def _fgate_kernel(x_ref, g_ref, w_ref, b_ref, o_ref, xn_ref, carry_ref, *, tiles_per_seq):
    i = pl.program_id(0)
    _norm_rows(x_ref, g_ref, xn_ref)
    y = _dot(xn_ref[...], w_ref[...]) + b_ref[...]
    lf = (jnp.minimum(y, 0.0) - jnp.log1p(jnp.exp(-jnp.abs(y)))) * LOG2E
    tm = lf.shape[0]
    tri = (lax.broadcasted_iota(jnp.int32, (tm, tm), 0) >= lax.broadcasted_iota(jnp.int32, (tm, tm), 1))
    tri = jnp.where(tri, 1.0, 0.0).astype(BF16)
    a = lf.astype(BF16)
    r1 = lf - a.astype(F32)
    b = r1.astype(BF16)
    c = (r1 - b.astype(F32)).astype(BF16)
    cs = _dot(tri, a) + _dot(tri, b) + _dot(tri, c)

    @pl.when(i % tiles_per_seq == 0)
    def _():
        carry_ref[...] = jnp.zeros_like(carry_ref)

    out = cs + carry_ref[...]
    o_ref[...] = out
    carry_ref[...] = out[tm - 1:tm, :]


def _fgate_cumsum(h2, gain, w_pad, b_pad, seq, tm=512):
    m, d = h2.shape
    tm = min(tm, seq)
    const = lambda i: (0, 0)
    return pl.pallas_call(
        functools.partial(_fgate_kernel, tiles_per_seq=seq // tm),
        grid=(m // tm,),
        in_specs=[pl.BlockSpec((tm, d), lambda i: (i, 0)), pl.BlockSpec((1, d), const),
                  pl.BlockSpec((d, LANES), const), pl.BlockSpec((1, LANES), const)],
        out_specs=pl.BlockSpec((tm, LANES), lambda i: (i, 0)),
        out_shape=jax.ShapeDtypeStruct((m, LANES), F32),
        scratch_shapes=[pltpu.VMEM((tm, d), BF16), pltpu.VMEM((1, LANES), F32)],
        compiler_params=_cparams(("arbitrary",)),
        name="fgate_cumsum",
    )(h2, gain.reshape(1, d), w_pad, b_pad)


def _oproj_kernel(h_ref, mix_ref, w_ref, o_ref):
    o_ref[...] = h_ref[...] + _dot(mix_ref[...], w_ref[...])


def _oproj(h2, mix2, w, tm=512):
    m, d = h2.shape
    tm = min(tm, m)
    return pl.pallas_call(
        _oproj_kernel,
        grid=(m // tm,),
        in_specs=[pl.BlockSpec((tm, d), lambda i: (i, 0)), pl.BlockSpec((tm, d), lambda i: (i, 0)),
                  pl.BlockSpec((d, d), lambda i: (0, 0))],
        out_specs=pl.BlockSpec((tm, d), lambda i: (i, 0)),
        out_shape=jax.ShapeDtypeStruct((m, d), F32),
        compiler_params=_cparams(("parallel",)),
        name="oproj",
    )(h2, mix2, w)


def _mlp_kernel(x_ref, g_ref, wu_ref, wd_ref, o_ref, xn_ref):
    f = pl.program_id(1)

    @pl.when(f == 0)
    def _():
        _norm_rows(x_ref, g_ref, xn_ref)
        o_ref[...] = x_ref[...]

    a = jnp.maximum(_dot(xn_ref[...], wu_ref[...]), 0.0)
    o_ref[...] += _dot((a * a).astype(BF16), wd_ref[...])


def _mlp(h2, gain, w_up, w_down, tm=512, tf=1024):
    m, d = h2.shape
    dff = w_up.shape[1]
    tm = min(tm, m)
    return pl.pallas_call(
        _mlp_kernel,
        grid=(m // tm, dff // tf),
        in_specs=[pl.BlockSpec((tm, d), lambda i, f: (i, 0)), pl.BlockSpec((1, d), lambda i, f: (0, 0)),
                  pl.BlockSpec((d, tf), lambda i, f: (0, f)), pl.BlockSpec((tf, d), lambda i, f: (f, 0))],
        out_specs=pl.BlockSpec((tm, d), lambda i, f: (i, 0)),
        out_shape=jax.ShapeDtypeStruct((m, d), F32),
        scratch_shapes=[pltpu.VMEM((tm, d), BF16)],
        compiler_params=_cparams(("parallel", "arbitrary")),
        name="mlp",
    )(h2, gain.reshape(1, d), w_up, w_down)


def _ple_kernel(x_ref, g_ref, wg_ref, p_ref, wp_ref, o_ref, xn_ref):
    _norm_rows(x_ref, g_ref, xn_ref)
    gate = jax.nn.sigmoid(_dot(xn_ref[...], wg_ref[...]))
    emb = _dot(p_ref[...].astype(BF16), wp_ref[...])
    o_ref[...] = x_ref[...] + gate * emb


def _ple(h2, gain, w_gate, p2, w_proj, tm=512):
    m, d = h2.shape
    pd = p2.shape[1]
    tm = min(tm, m)
    const = lambda i: (0, 0)
    return pl.pallas_call(
        _ple_kernel,
        grid=(m // tm,),
        in_specs=[pl.BlockSpec((tm, d), lambda i: (i, 0)), pl.BlockSpec((1, d), const),
                  pl.BlockSpec((d, d), const), pl.BlockSpec((tm, pd), lambda i: (i, 0)),
                  pl.BlockSpec((pd, d), const)],
        out_specs=pl.BlockSpec((tm, d), lambda i: (i, 0)),
        out_shape=jax.ShapeDtypeStruct((m, d), F32),
        scratch_shapes=[pltpu.VMEM((tm, d), BF16)],
        compiler_params=_cparams(("parallel",)),
        name="ple",
    )(h2, gain.reshape(1, d), w_gate, p2, w_proj)


def _sb_kernel(q_ref, k_ref, v_ref, o_ref, acc_ref, aft_ref, *, tq, heads):
    i = pl.program_id(2)
    row = lax.broadcasted_iota(jnp.int32, (tq, tq), 0)
    col = lax.broadcasted_iota(jnp.int32, (tq, tq), 1)
    later = jnp.where(row > col, 1.0, 0.0).astype(BF16)
    strict = col < row

    def block(hd, j, diagonal):
        start = pl.multiple_of(j * tq, tq)
        cols = slice(hd * HEAD_DIM, (hd + 1) * HEAD_DIM)
        kb = k_ref[0, pl.ds(start, tq), cols]
        vb = v_ref[0, pl.ds(start, tq), cols]
        z = _nt_dot(q_ref[0, :, cols], kb)
        sp = jnp.maximum(z, 0.0) + jnp.log(1.0 + jnp.exp(-jnp.abs(z)))
        log_1mb = -sp
        if diagonal:
            log_1mb = jnp.where(strict, log_1mb, 0.0)
        hi = log_1mb.astype(BF16)
        lo = (log_1mb - hi.astype(F32)).astype(BF16)
        log_w = (z - sp) + _dot(hi, later) + _dot(lo, later)
        return log_w, jnp.sum(log_1mb, axis=-1, keepdims=True), vb

    prev = jnp.maximum(i - 1, 0)
    has_prev = (i > 0).astype(F32)
    worst = None
    for hd in range(heads):
        log_w0, sum0, v0 = block(hd, i, True)
        log_w1, sum1, v1 = block(hd, prev, False)
        w0 = jnp.where(strict, jnp.exp(log_w0), 0.0)
        w1 = jnp.exp(log_w1 + sum0) * has_prev
        acc_ref[hd] = _dot(w0.astype(BF16), v0) + _dot(w1.astype(BF16), v1)
        aft = sum0 + sum1 * has_prev
        aft_ref[hd] = aft
        worst = jnp.max(aft) if worst is None else jnp.maximum(worst, jnp.max(aft))

    def cond(c):
        j, worst = c
        return jnp.logical_and(j >= 0, worst > EXP_UNDERFLOW)

    def body(c):
        j, _ = c
        worst = None
        for hd in range(heads):
            log_w, row_sum, vb = block(hd, j, False)
            acc_ref[hd] += _dot(jnp.exp(log_w + aft_ref[hd]).astype(BF16), vb)
            aft = aft_ref[hd] + row_sum
            aft_ref[hd] = aft
            worst = jnp.max(aft) if worst is None else jnp.maximum(worst, jnp.max(aft))
        return j - 1, worst

    lax.while_loop(cond, body, (i - 2, worst))
    for hd in range(heads):
        o_ref[0, :, hd * HEAD_DIM:(hd + 1) * HEAD_DIM] = acc_ref[hd].astype(o_ref.dtype)


def _sb_attention(qkv, tq=256, heads=4):
    bsz, seq, _ = qkv.shape
    tq = min(tq, seq)
    width = heads * HEAD_DIM
    n_groups = N_HEADS // heads
    return pl.pallas_call(
        functools.partial(_sb_kernel, tq=tq, heads=heads),
        grid=(bsz, n_groups, seq // tq),
        in_specs=[pl.BlockSpec((1, tq, width), lambda b, h, i: (b, i, h)),
                  pl.BlockSpec((1, seq, width), lambda b, h, i: (b, 0, n_groups + h)),
                  pl.BlockSpec((1, seq, width), lambda b, h, i: (b, 0, 2 * n_groups + h))],
        out_specs=pl.BlockSpec((1, tq, width), lambda b, h, i: (b, i, h)),
        out_shape=jax.ShapeDtypeStruct((bsz, seq, N_HEADS * HEAD_DIM), BF16),
        scratch_shapes=[pltpu.VMEM((heads, tq, HEAD_DIM), F32), pltpu.VMEM((heads, tq, 1), F32)],
        compiler_params=_cparams(("parallel", "parallel", "arbitrary")),
        name="sb_attn",
    )(qkv, qkv, qkv)


def _tile_lanes(x, n):
    return x if n == 1 else jnp.concatenate([x] * n, axis=1)


def _fox_first_block_kernel(c0_ref, ce_ref, bound_ref, o_ref):
    c0 = c0_ref[...]
    n_blocks = ce_ref.shape[1]
    first = jnp.full(c0.shape, n_blocks, jnp.int32)
    for j in range(n_blocks):
        needed = c0 - ce_ref[:, j:j + 1] >= -bound_ref[...]
        first = jnp.minimum(first, jnp.where(needed, j, n_blocks))
    o_ref[...] = first


def _fox_first_block(cum_t, tq, tk, qk_bound):
    bsz, n_heads, _, seq = cum_t.shape
    rows = bsz * n_heads
    c0 = cum_t[:, :, 0, ::tq].reshape(rows, seq // tq)
    ce = cum_t[:, :, 0, tk - 1::tk].reshape(rows, seq // tk)
    bound = (151.0 + 2.04 * qk_bound).astype(F32).reshape(1, 1)
    full = lambda shape: pl.BlockSpec(shape, lambda: (0,) * len(shape))
    return pl.pallas_call(
        _fox_first_block_kernel,
        in_specs=[full(c0.shape), full(ce.shape), full((1, 1))],
        out_specs=full(c0.shape),
        out_shape=jax.ShapeDtypeStruct(c0.shape, jnp.int32),
        name="fox_first_block",
    )(c0, ce, bound)


def _fox_kernel(first_ref, q_ref, k_ref, vt_ref, cum_ref, cumt_ref, o_ref, acc_ref, m_ref, csb_ref, *, tq, tk):
    h = pl.program_id(1)
    i = pl.program_id(2)
    seq = k_ref.shape[1]
    ratio = tq // tk

    @pl.when(i == 0)
    def _():
        lane = lax.broadcasted_iota(jnp.int32, (tk, LANES), 1)

        def fill(c, carry):
            st = pl.multiple_of(c * tk, tk)
            col = jnp.sum(jnp.where(lane == h, cum_ref[0, pl.ds(st, tk), :], 0.0), axis=-1, keepdims=True)
            csb_ref[pl.ds(st, tk), :] = jnp.broadcast_to(col, (tk, LANES))
            return carry

        lax.fori_loop(0, seq // tk, fill, 0)

    q = q_ref[0]
    ct = cumt_ref[0, 0, :, pl.ds(pl.multiple_of(i * tq, tq), tq)]
    key_pos = lax.broadcasted_iota(jnp.int32, (tk, tq), 0)
    query_pos = lax.broadcasted_iota(jnp.int32, (tk, tq), 1)

    def scores(j, diagonal=None):
        st = pl.multiple_of(j * tk, tk)
        zt = _nt_dot(k_ref[0, pl.ds(st, tk), :], q) + ct - _tile_lanes(csb_ref[pl.ds(st, tk), :], tq // LANES)
        if diagonal is not None:
            zt = jnp.where(key_pos + diagonal * tk <= query_pos, zt, NEG_BIG)
        return zt.astype(BF16)

    def update(j, zt):
        vtb = vt_ref[0, 0, :, pl.ds(pl.multiple_of(j * tk, tk), tk)]
        m_old = m_ref[...]
        m_new = jnp.maximum(m_old, jnp.max(zt, axis=0, keepdims=True).astype(F32))
        p = jnp.exp2(zt - m_new.astype(BF16))
        acc_ref[...] = jnp.exp2(m_old - m_new) * acc_ref[...] + _dot(vtb, p)
        m_ref[...] = m_new

    m_ref[...] = jnp.full_like(m_ref, NEG_BIG)
    acc_ref[...] = jnp.zeros_like(acc_ref)

    below = i * ratio
    first = jnp.minimum(first_ref[pl.program_id(0) * N_HEADS + h, i], below)
    n_full = below - first

    def pair(jj, carry):
        ja = first + 2 * jj
        za = scores(ja)
        zb = scores(ja + 1)
        update(ja, za)
        update(ja + 1, zb)
        return carry

    lax.fori_loop(0, n_full // 2, pair, 0)

    @pl.when(n_full % 2 == 1)
    def _():
        update(below - 1, scores(below - 1))

    for d in range(ratio):
        update(below + d, scores(below + d, d))
    o_ref[0] = jnp.transpose(acc_ref[0:HEAD_DIM, :] / acc_ref[HEAD_DIM:HEAD_DIM + 1, :]).astype(o_ref.dtype)


def _fox_attention(qkv, v_t, cum, cum_t, qk_bound, tq=1024, tk=512):
    bsz, seq, _ = qkv.shape
    tq = min(tq, seq)
    tk = min(tk, tq)
    dv = HEAD_DIM + ONES_ROWS
    first = _fox_first_block(cum_t, tq, tk, qk_bound)
    grid_spec = pltpu.PrefetchScalarGridSpec(
        num_scalar_prefetch=1,
        grid=(bsz, N_HEADS, seq // tq),
        in_specs=[pl.BlockSpec((1, tq, HEAD_DIM), lambda b, h, i, f: (b, i, h)),
                  pl.BlockSpec((1, seq, HEAD_DIM), lambda b, h, i, f: (b, 0, N_HEADS + h)),
                  pl.BlockSpec((1, 1, dv, seq), lambda b, h, i, f: (b, h, 0, 0)),
                  pl.BlockSpec((1, seq, LANES), lambda b, h, i, f: (b, 0, 0)),
                  pl.BlockSpec((1, 1, 1, seq), lambda b, h, i, f: (b, h, 0, 0))],
        out_specs=pl.BlockSpec((1, tq, HEAD_DIM), lambda b, h, i, f: (b, i, h)),
        scratch_shapes=[pltpu.VMEM((dv, tq), F32), pltpu.VMEM((1, tq), F32), pltpu.VMEM((seq, LANES), F32)])
    return pl.pallas_call(
        functools.partial(_fox_kernel, tq=tq, tk=tk),
        grid_spec=grid_spec,
        out_shape=jax.ShapeDtypeStruct((bsz, seq, N_HEADS * HEAD_DIM), BF16),
        compiler_params=_cparams(("parallel", "parallel", "arbitrary")),
        name="fox_attn",
    )(first, qkv, qkv, v_t, cum, cum_t)


def _sortable(x):
    bits = pltpu.bitcast(x, jnp.int32)
    return jnp.where(bits < 0, bits ^ jnp.int32(0x7FFFFFFF), bits)


def _dsa_kernel(q_ref, k_ref, vt_ref, qi_ref, ki_ref, wi_ref, o_ref,
                key_ref, q4_ref, thr_ref, acc_ref, m_ref, *, t, k_top, count_rows):
    i = pl.program_id(1)
    n_blocks = i + 1
    group = N_HEADS // B_KV_HEADS
    qs = i * t

    for g in range(B_KV_HEADS):
        q4_ref[g] = jnp.concatenate(
            [q_ref[0, :, (g * group + r) * HEAD_DIM:(g * group + r + 1) * HEAD_DIM] for r in range(group)], axis=0)

    key_pos = lax.broadcasted_iota(jnp.int32, (t, t), 0)
    chunk_end = ((qs + lax.broadcasted_iota(jnp.int32, (1, t), 1)) // CHUNK + 1) * CHUNK

    def score_block(j, carry):
        st = pl.multiple_of(j * t, t)
        ka = ki_ref[0, pl.ds(st, t), 0:LANES]
        kb = ki_ref[0, pl.ds(st, t), LANES:2 * LANES]
        score = jnp.zeros((t, t), F32)
        for mblk in range(IDX_HEADS // 2):
            qm = qi_ref[0, :, mblk * LANES:(mblk + 1) * LANES]
            for half, kk in enumerate((ka, kb)):
                hd = 2 * mblk + half
                score = score + jnp.maximum(_nt_dot(kk, qm), 0.0) * wi_ref[0, hd:hd + 1, :]
        admissible = (st + key_pos) < chunk_end
        key_ref[pl.ds(st, t), :] = jnp.where(admissible, _sortable(score), KEY_NEG_INF)
        return carry

    lax.fori_loop(0, n_blocks, score_block, 0)

    def bit_body(it, lo):
        cand = lo + jnp.left_shift(jnp.int32(1), 31 - it)

        def count_body(j, cnt):
            st = pl.multiple_of(j * t, t)
            for u in range(t // count_rows):
                cnt = cnt + jnp.where(key_ref[pl.ds(st + u * count_rows, count_rows), :] >= cand, 1.0, 0.0)
            return cnt

        cnt = lax.fori_loop(0, n_blocks, count_body, jnp.zeros((count_rows, t), F32))
        total = jnp.sum(cnt, axis=0, keepdims=True)
        return jnp.where(total >= k_top, cand, lo)

    lo = lax.fori_loop(0, 32, bit_body, jnp.full((1, t), INT_MIN, jnp.int32))
    thr_ref[...] = jnp.maximum(lo, KEY_NEG_INF + 1)

    m_ref[...] = jnp.full_like(m_ref, NEG_BIG)
    acc_ref[...] = jnp.zeros_like(acc_ref)

    def attn_block(j, carry):
        st = pl.multiple_of(j * t, t)
        bias = jnp.where(key_ref[pl.ds(st, t), :] >= thr_ref[...], 0.0, NEG_BIG).astype(BF16)
        bias4 = _tile_lanes(bias, group)
        scores = [_nt_dot(k_ref[0, pl.ds(st, t), g * HEAD_DIM:(g + 1) * HEAD_DIM], q4_ref[g]).astype(BF16) + bias4
                  for g in range(B_KV_HEADS)]
        for g in range(B_KV_HEADS):
            vtb = vt_ref[0, g, :, pl.ds(st, t)]
            zt = scores[g]
            m_old = m_ref[g]
            m_new = jnp.maximum(m_old, jnp.max(zt, axis=0, keepdims=True).astype(F32))
            p = jnp.exp2(zt - m_new.astype(BF16))
            acc_ref[g] = jnp.exp2(m_old - m_new) * acc_ref[g] + _dot(vtb, p)
            m_ref[g] = m_new
        return carry

    lax.fori_loop(0, n_blocks, attn_block, 0)
    for g in range(B_KV_HEADS):
        out_t = acc_ref[g, 0:HEAD_DIM, :] / acc_ref[g, HEAD_DIM:HEAD_DIM + 1, :]
        for r in range(group):
            hd = g * group + r
            o_ref[0, :, hd * HEAD_DIM:(hd + 1) * HEAD_DIM] = jnp.transpose(
                out_t[:, r * t:(r + 1) * t]).astype(o_ref.dtype)


def _dsa_attention(qkv, v_t, qi, ki2, wi_t, k_top, t=256, count_rows=64):
    bsz, seq, _ = qkv.shape
    t = min(t, seq)
    count_rows = min(count_rows, t)
    dq = N_HEADS * HEAD_DIM
    dkv = B_KV_HEADS * HEAD_DIM
    dv = HEAD_DIM + ONES_ROWS
    group = N_HEADS // B_KV_HEADS
    resident = dict(pipeline_mode=pl.Buffered(1))
    return pl.pallas_call(
        functools.partial(_dsa_kernel, t=t, k_top=k_top, count_rows=count_rows),
        grid=(bsz, seq // t),
        in_specs=[pl.BlockSpec((1, t, dq), lambda b, i: (b, i, 0)),
                  pl.BlockSpec((1, seq, dkv), lambda b, i: (b, 0, dq // dkv), **resident),
                  pl.BlockSpec((1, B_KV_HEADS, dv, seq), lambda b, i: (b, 0, 0, 0), **resident),
                  pl.BlockSpec((1, t, IDX_HEADS * IDX_DIM), lambda b, i: (b, i, 0)),
                  pl.BlockSpec((1, seq, 2 * LANES), lambda b, i: (b, 0, 0), **resident),
                  pl.BlockSpec((1, IDX_HEADS, t), lambda b, i: (b, 0, i))],
        out_specs=pl.BlockSpec((1, t, dq), lambda b, i: (b, i, 0)),
        out_shape=jax.ShapeDtypeStruct((bsz, seq, dq), BF16),
        scratch_shapes=[pltpu.VMEM((seq, t), jnp.int32),
                        pltpu.VMEM((B_KV_HEADS, group * t, HEAD_DIM), BF16),
                        pltpu.VMEM((1, t), jnp.int32),
                        pltpu.VMEM((B_KV_HEADS, dv, group * t), F32),
                        pltpu.VMEM((B_KV_HEADS, 1, group * t), F32)],
        compiler_params=_cparams(("parallel", "arbitrary")),
        name="dsa_attn",
    )(qkv, qkv, v_t, qi, ki2, wi_t)


def _rope_tables(seq, rot_dim, period, active_lanes):
    half = rot_dim // 2
    inv = ROPE_THETA ** (-jnp.arange(half, dtype=F32) / half)
    ang = jnp.arange(seq, dtype=F32)[:, None] * inv[None, :]
    cos, sin = jnp.cos(ang), jnp.sin(ang)
    lane = jnp.arange(LANES)
    local = lane % period
    idx = local % half
    first = (local < half) & (lane < active_lanes)
    second = (local >= half) & (local < rot_dim) & (lane < active_lanes)
    c = jnp.where((first | second)[None, :], cos[:, idx], 1.0)
    s_up = jnp.where(first[None, :], -sin[:, idx], 0.0)
    s_dn = jnp.where(second[None, :], sin[:, idx], 0.0)
    return c, s_up, s_dn


def _rope_partner_matrix(rot_dim, period, n_lanes):
    half = rot_dim // 2
    lane = jnp.arange(n_lanes)
    local = lane % period
    src = lane[:, None]
    dst = lane[None, :]
    first = (local < half)[None, :] & (src == dst + half)
    second = ((local >= half) & (local < rot_dim))[None, :] & (src == dst - half)
    return (jnp.where(second, 1.0, 0.0) - jnp.where(first, 1.0, 0.0)).astype(BF16)


def _transpose_v(v, n_heads):
    bsz, seq, _ = v.shape
    v_t = jnp.transpose(v.reshape(bsz, seq, n_heads, HEAD_DIM), (0, 2, 3, 1))
    return jnp.concatenate([v_t, jnp.ones((bsz, n_heads, ONES_ROWS, seq), v.dtype)], axis=2)


def _pad_lanes(v, fill=0.0):
    return jnp.pad(v.astype(F32), (0, LANES - v.shape[0]), constant_values=fill).reshape(1, LANES)


def kernel(x, p, attn_norm, w_o, mlp_norm, w_up, w_down, ple_norm, w_ple_gate, w_ple_proj,
           a_w_in, a_q_norm, a_k_norm, b_w_in, b_q_norm, b_k_norm, b_idx_k_norm,
           c_w_in, c_b_f, c_q_norm, c_k_norm):
    bsz, seq, d = x.shape
    depth = attn_norm.shape[0]
    m = bsz * seq
    dm = N_HEADS * HEAD_DIM
    dkv = B_KV_HEADS * HEAD_DIM
    n_qi = IDX_HEADS * IDX_DIM
    scale = HEAD_DIM ** -0.5
    k_top = min(TOPK_MAX, seq // 4)

    h = x.reshape(m, d)
    for layer in range(depth):
        kind, j = layer % N_MIXERS, layer // N_MIXERS
        if kind == 0:
            gains = jnp.stack([a_q_norm[j] * scale, a_k_norm[j]])
            nt = dm // 1024
            qkv = _proj(h, attn_norm[layer], a_w_in[j].astype(BF16), gains, nt, nt, 1024)
            mix = _sb_attention(qkv.reshape(bsz, seq, 3 * dm))
        elif kind == 1:
            w = b_w_in[j]
            gains = jnp.stack([b_q_norm[j] * (scale * LOG2E), b_k_norm[j]])
            cos_tab, s_up, s_dn = _rope_tables(seq, ROPE_DIM, HEAD_DIM, LANES)
            main_tabs = (cos_tab, s_dn - s_up, _rope_partner_matrix(ROPE_DIM, HEAD_DIM, 2 * LANES))
            tn = dkv
            qkv = _proj(h, attn_norm[layer], w[:, :dm + 2 * dkv].astype(BF16), gains, dm // tn, 1, tn,
                        rope_tabs=main_tabs, rope_half=ROPE_DIM // 2, seq=seq)
            tail_w = IDX_DIM + IDX_HEADS
            w_idx = jnp.pad(w[:, dm + 2 * dkv:], ((0, 0), (0, LANES - tail_w))).astype(BF16)
            idx_tabs = _rope_tables(seq, IDX_ROPE_DIM, IDX_DIM, LANES)
            tail_tabs = _rope_tables(seq, IDX_ROPE_DIM, IDX_DIM, IDX_DIM)
            qi, tail = _proj_idx(h, attn_norm[layer], w_idx, _pad_lanes(b_idx_k_norm[j]),
                                 idx_tabs + tail_tabs, seq)
            ki = tail[:, :IDX_DIM].astype(BF16)
            zeros = jnp.zeros((m, LANES - IDX_DIM), BF16)
            ki2 = jnp.concatenate([ki, zeros, zeros, ki], axis=1).reshape(bsz, seq, 2 * LANES)
            wi_t = jnp.transpose(tail[:, IDX_DIM:tail_w].reshape(bsz, seq, IDX_HEADS), (0, 2, 1))
            qkv = qkv.reshape(bsz, seq, dm + 2 * dkv)
            v_t = _transpose_v(qkv[:, :, dm + dkv:], B_KV_HEADS)
            mix = _dsa_attention(qkv, v_t, qi.reshape(bsz, seq, n_qi), ki2, wi_t, k_top)
        else:
            w = c_w_in[j]
            gains = jnp.stack([c_q_norm[j] * (scale * LOG2E), c_k_norm[j]])
            nt = dm // 1024
            qkv = _proj(h, attn_norm[layer], w[:, :3 * dm].astype(BF16), gains, nt, nt, 1024)
            w_f = jnp.pad(w[:, 3 * dm:], ((0, 0), (0, LANES - N_HEADS))).astype(BF16)
            cum = _fgate_cumsum(h, attn_norm[layer], w_f, _pad_lanes(c_b_f[j]), seq)
            cum3 = cum.reshape(bsz, seq, LANES)
            cum_t = jnp.transpose(cum3[:, :, :N_HEADS], (0, 2, 1)).reshape(bsz, N_HEADS, 1, seq)
            qkv = qkv.reshape(bsz, seq, 3 * dm)
            v_t = _transpose_v(qkv[:, :, 2 * dm:], N_HEADS)
            qk_bound = (HEAD_DIM * scale * LOG2E) * jnp.max(jnp.abs(c_q_norm[j])) * jnp.max(jnp.abs(c_k_norm[j]))
            mix = _fox_attention(qkv, v_t, cum3, cum_t, qk_bound)
        h = _oproj(h, mix.reshape(m, dm), w_o[layer].astype(BF16))
        h = _mlp(h, mlp_norm[layer], w_up[layer].astype(BF16), w_down[layer].astype(BF16))
        h = _ple(h, ple_norm[layer], w_ple_gate[layer].astype(BF16), p[layer].reshape(m, -1),
                 w_ple_proj[layer].astype(BF16))
    return h.reshape(bsz, seq, d)
```
